```python
import jax, jax.numpy as jnp
from jax import lax
import numpy as np

D_MODEL = 1024
BATCH = 8
SEQ = 2048
DEPTH = 4

CHUNK = 64
N_MIXERS = 3
A_HEADS = 16
A_HEAD_DIM = D_MODEL // A_HEADS
A_Q_BLOCK = 128
B_HEADS = 8
B_HEAD_DIM = D_MODEL // B_HEADS
B_CONV = 4
C_WIDTH = 2 * D_MODEL
C_GROUPS = 8
C_SPAN = 128
D_FF = -(-8 * D_MODEL // (3 * 256)) * 256
EPS = 1e-6
N_A = (DEPTH + 2) // 3
N_B = (DEPTH + 1) // 3
N_C = DEPTH // 3

kernel_name = 'fox_mlstm_gmlp_interleaved_trunk'


def rmsnorm(x, g):
    xf = x.astype(jnp.float32)
    y = xf * lax.rsqrt(jnp.mean(xf * xf, axis=-1, keepdims=True) + EPS)
    return (y * g.astype(jnp.float32)).astype(x.dtype)


def forgetting_attention(h, w_in, b_f, w_out):
    B, S, D = h.shape
    proj = h @ w_in
    q, k, v, f = jnp.split(proj, [D, 2 * D, 3 * D], axis=-1)
    to_heads = lambda t: t.reshape(B, S, A_HEADS, A_HEAD_DIM).transpose(0, 2, 1, 3)
    q, k, v = to_heads(q), to_heads(k), to_heads(v)
    logf = jax.nn.log_sigmoid((f + b_f).astype(jnp.float32))
    c = jnp.cumsum(logf, axis=1).transpose(0, 2, 1)
    scale = A_HEAD_DIM ** -0.5
    outs = []
    for j in range(S // A_Q_BLOCK):
        q0, q1 = j * A_Q_BLOCK, (j + 1) * A_Q_BLOCK
        s = jnp.einsum('bhqd,bhkd->bhqk', q[:, :, q0:q1], k[:, :, :q1]).astype(jnp.float32) * scale
        s = s + c[:, :, q0:q1, None] - c[:, :, None, :q1]
        mask = (q0 + jnp.arange(A_Q_BLOCK))[:, None] >= jnp.arange(q1)[None, :]
        p = jax.nn.softmax(jnp.where(mask, s, -jnp.inf), axis=-1)
        outs.append(jnp.einsum('bhqk,bhkd->bhqd', p.astype(v.dtype), v[:, :, :q1]))
    o = jnp.concatenate(outs, axis=2).transpose(0, 2, 1, 3).reshape(B, S, D)
    return o @ w_out


def causal_dwconv(x, w, b):
    K = w.shape[0]
    y = lax.conv_general_dilated(x, w[:, None, :], window_strides=(1,), padding=[(K - 1, 0)],
                                 dimension_numbers=('NWC', 'WIO', 'NWC'),
                                 feature_group_count=x.shape[-1])
    return y + b


def mlstm_mixer(h, w_in, conv_w, conv_b, b_i, b_f, norm_g, w_out):
    B, S, D = h.shape
    H, d, L = B_HEADS, B_HEAD_DIM, CHUNK
    NC = S // L
    proj = h @ w_in
    qk, v, o, ig, fg = jnp.split(proj, [2 * D, 3 * D, 4 * D, 4 * D + H], axis=-1)
    qk = jax.nn.silu(causal_dwconv(qk, conv_w, conv_b))
    q, k = jnp.split(qk, 2, axis=-1)
    to_chunks = lambda t: t.reshape(B, NC, L, H, d).transpose(1, 0, 3, 2, 4).astype(jnp.float32)
    q, k, v = to_chunks(q), to_chunks(k) * (d ** -0.5), to_chunks(v)
    gate_chunks = lambda t: t.astype(jnp.float32).reshape(B, NC, L, H).transpose(1, 0, 3, 2)
    ig = gate_chunks(ig + b_i)
    lf = gate_chunks(jax.nn.log_sigmoid((fg + b_f).astype(jnp.float32)))
    tri = jnp.arange(L)[:, None] >= jnp.arange(L)[None, :]

    def step(carry, inp):
        C, n, m = carry
        qc, kc, vc, igc, lfc = inp
        bcum = jnp.cumsum(lfc, axis=-1)
        log_D = jnp.where(tri, bcum[..., :, None] - bcum[..., None, :] + igc[..., None, :], -jnp.inf)
        m_inter = bcum + m[..., None]
        m_t = jnp.maximum(m_inter, jnp.max(log_D, axis=-1))
        Dm = jnp.exp(log_D - m_t[..., None])
        a = jnp.exp(m_inter - m_t)
        sqk = jnp.einsum('bhtd,bhsd->bhts', qc, kc) * Dm
        num = a[..., None] * jnp.einsum('bhtd,bhde->bhte', qc, C) + jnp.einsum('bhts,bhse->bhte', sqk, vc)
        den = a * jnp.einsum('bhtd,bhd->bht', qc, n) + jnp.sum(sqk, axis=-1)
        hc = num / jnp.maximum(jnp.abs(den), jnp.exp(-m_t))[..., None]
        bL = bcum[..., -1]
        g = bL[..., None] - bcum + igc
        m_new = jnp.maximum(bL + m, jnp.max(g, axis=-1))
        w = jnp.exp(g - m_new[..., None])
        decay = jnp.exp(bL + m - m_new)
        C = decay[..., None, None] * C + jnp.einsum('bhsd,bhse->bhde', kc * w[..., None], vc)
        n = decay[..., None] * n + jnp.einsum('bhs,bhsd->bhd', w, kc)
        return (C, n, m_new), hc

    init = (jnp.zeros((B, H, d, d), jnp.float32), jnp.zeros((B, H, d), jnp.float32),
            jnp.zeros((B, H), jnp.float32))
    _, hs = lax.scan(step, init, (q, k, v, ig, lf))
    hs = hs.transpose(1, 0, 3, 2, 4).reshape(B, S, H, d)
    hs = hs * lax.rsqrt(jnp.mean(hs * hs, axis=-1, keepdims=True) + EPS)
    hs = hs.reshape(B, S, D) * norm_g.astype(jnp.float32)
    out = hs.astype(h.dtype) * jax.nn.sigmoid(o)
    return out @ w_out


def gmlp_mixer(h, w_in, ln_g, ln_b, w_s, b_s, w_out):
    B, S, D = h.shape
    z = jax.nn.gelu(h @ w_in, approximate=False)
    u, v = jnp.split(z, 2, axis=-1)
    vf = v.astype(jnp.float32)
    mu = jnp.mean(vf, axis=-1, keepdims=True)
    var = jnp.mean(jnp.square(vf - mu), axis=-1, keepdims=True)
    v = ((vf - mu) * lax.rsqrt(var + EPS) * ln_g + ln_b).astype(h.dtype)
    v = v.reshape(B, S // C_SPAN, C_SPAN, C_GROUPS, C_WIDTH // C_GROUPS)
    pos = jnp.arange(C_SPAN)
    mask = (pos[:, None] // CHUNK) >= (pos[None, :] // CHUNK)
    ws = w_s * mask
    s = jnp.einsum('gts,bnsgc->bntgc', ws, v) + b_s.T[None, None, :, :, None]
    out = u * s.reshape(B, S, C_WIDTH)
    return out @ w_out


def swiglu(h, w_gu, w_down):
    g, u = jnp.split(h @ w_gu, 2, axis=-1)
    return (jax.nn.silu(g) * u) @ w_down


def setup_inputs(seed: int = 0) -> dict:
    key = jax.random.key(seed)
    ks = iter(jax.random.split(key, 32))
    nrm = lambda shape, scale: jax.random.normal(next(ks), shape, jnp.float32) * scale
    D = D_MODEL
    return {
        'x': nrm((BATCH, SEQ, D), 1.0),
        'norm1_g': 1.0 + nrm((DEPTH, D), 0.02),
        'norm2_g': 1.0 + nrm((DEPTH, D), 0.02),
        'final_g': 1.0 + nrm((D,), 0.02),
        'a_w_in': nrm((N_A, D, 3 * D + A_HEADS), D ** -0.5),
        'a_b_f': jnp.linspace(1.0, 5.0, A_HEADS)[None, :] + nrm((N_A, A_HEADS), 0.1),
        'a_w_out': nrm((N_A, D, D), D ** -0.5),
        'b_w_in': nrm((N_B, D, 4 * D + 2 * B_HEADS), D ** -0.5),
        'b_conv_w': nrm((N_B, B_CONV, 2 * D), B_CONV ** -0.5),
        'b_conv_b': nrm((N_B, 2 * D), 0.01),
        'b_b_i': nrm((N_B, B_HEADS), 0.1),
        'b_b_f': jnp.linspace(3.0, 6.0, B_HEADS)[None, :] + nrm((N_B, B_HEADS), 0.1),
        'b_norm_g': 1.0 + nrm((N_B, D), 0.02),
        'b_w_out': nrm((N_B, D, D), D ** -0.5),
        'c_w_in': nrm((N_C, D, 2 * C_WIDTH), D ** -0.5),
        'c_ln_g': 1.0 + nrm((N_C, C_WIDTH), 0.02),
        'c_ln_b': nrm((N_C, C_WIDTH), 0.01),
        'c_w_s': nrm((N_C, C_GROUPS, C_SPAN, C_SPAN), C_SPAN ** -0.5),
        'c_b_s': 1.0 + nrm((N_C, C_GROUPS, C_SPAN), 0.02),
        'c_w_out': nrm((N_C, C_WIDTH, D), C_WIDTH ** -0.5),
        'ffn_w_gu': nrm((DEPTH, D, 2 * D_FF), D ** -0.5),
        'ffn_w_down': nrm((DEPTH, D_FF, D), D_FF ** -0.5),
    }


def reference(x, norm1_g, norm2_g, final_g, a_w_in, a_b_f, a_w_out, b_w_in, b_conv_w, b_conv_b,
              b_b_i, b_b_f, b_norm_g, b_w_out, c_w_in, c_ln_g, c_ln_b, c_w_s, c_b_s, c_w_out,
              ffn_w_gu, ffn_w_down):
    for i in range(DEPTH):
        kind, j = i % N_MIXERS, i // N_MIXERS
        hn = rmsnorm(x, norm1_g[i])
        if kind == 0:
            y = forgetting_attention(hn, a_w_in[j], a_b_f[j], a_w_out[j])
        elif kind == 1:
            y = mlstm_mixer(hn, b_w_in[j], b_conv_w[j], b_conv_b[j], b_b_i[j], b_b_f[j],
                            b_norm_g[j], b_w_out[j])
        else:
            y = gmlp_mixer(hn, c_w_in[j], c_ln_g[j], c_ln_b[j], c_w_s[j], c_b_s[j], c_w_out[j])
        x = x + y
        x = x + swiglu(rmsnorm(x, norm2_g[i]), ffn_w_gu[i], ffn_w_down[i])
    return rmsnorm(x, final_g)
```

```python
import functools
import math

import jax
import jax.numpy as jnp
from jax import lax
from jax.experimental import pallas as pl
from jax.experimental.pallas import tpu as pltpu

F32 = jnp.float32
BF16 = jnp.bfloat16

EPS = 1e-6
LANES = 128
A_HEADS = 16
A_HEAD_DIM = 64
B_HEADS = 8
B_HEAD_DIM = 128
B_CONV = 4
C_GROUPS = 8
C_SPAN = 128
CHUNK = 64
MLSTM_CHUNK = 64
VMEM_LIMIT = 56 * 1024 * 1024


def _params(*sem):
    return pltpu.CompilerParams(dimension_semantics=sem, vmem_limit_bytes=VMEM_LIMIT)


def _resident(shape):
    nd = len(shape)
    return pl.BlockSpec(shape, lambda *_: (0,) * nd, pipeline_mode=pl.Buffered(1))


def _rmsnorm(x, g):
    ms = jnp.mean(x * x, axis=-1, keepdims=True)
    return x * lax.rsqrt(ms + EPS) * g


def _sigmoid(x):
    return 1.0 / (1.0 + jnp.exp(-x))


def _log_sigmoid(z):
    return jnp.minimum(z, 0.0) - jnp.log1p(jnp.exp(-jnp.abs(z)))


def _gelu_exact(x):
    return 0.5 * x * (1.0 + lax.erf(x * (1.0 / math.sqrt(2.0))))


def _norm_proj_kernel(x_ref, g_ref, w_ref, wg_ref, o_ref, og_ref, *, col_chunk):
    hn = _rmsnorm(x_ref[...], g_ref[...]).astype(BF16)
    n = w_ref.shape[1]
    for c0 in range(0, n, col_chunk):
        o_ref[:, c0:c0 + col_chunk] = jnp.dot(
            hn, w_ref[:, c0:c0 + col_chunk], preferred_element_type=F32).astype(o_ref.dtype)
    og_ref[...] = jnp.dot(hn, wg_ref[...], preferred_element_type=F32)


def _norm_proj(x2, g, w, wg, *, tm=512, col_chunk=512):
    t, d = x2.shape
    n = w.shape[1]
    return pl.pallas_call(
        functools.partial(_norm_proj_kernel, col_chunk=col_chunk),
        grid=(t // tm,),
        in_specs=[pl.BlockSpec((tm, d), lambda i: (i, 0)),
                  _resident((1, d)), _resident((d, n)), _resident((d, LANES))],
        out_specs=[pl.BlockSpec((tm, n), lambda i: (i, 0)),
                   pl.BlockSpec((tm, LANES), lambda i: (i, 0))],
        out_shape=[jax.ShapeDtypeStruct((t, n), BF16),
                   jax.ShapeDtypeStruct((t, LANES), F32)],
        compiler_params=_params("parallel"),
        name="norm_proj",
    )(x2, g.reshape(1, d), w, wg)


def _matmul_res_kernel(a_ref, w_ref, x_ref, o_ref):
    o_ref[...] = x_ref[...] + jnp.dot(a_ref[...], w_ref[...], preferred_element_type=F32)


def _matmul_res(a, w, x2, *, tm=512):
    t, k = a.shape
    d = w.shape[1]
    return pl.pallas_call(
        _matmul_res_kernel,
        grid=(t // tm,),
        in_specs=[pl.BlockSpec((tm, k), lambda i: (i, 0)), _resident((k, d)),
                  pl.BlockSpec((tm, d), lambda i: (i, 0))],
        out_specs=pl.BlockSpec((tm, d), lambda i: (i, 0)),
        out_shape=jax.ShapeDtypeStruct((t, d), F32),
        compiler_params=_params("parallel"),
        name="matmul_res",
    )(a, w, x2)


def _ffn_kernel(x_ref, g_ref, wg_ref, wu_ref, wd_ref, fg_ref, o_ref, *, ff_chunk, final_norm):
    x = x_ref[...]
    hn = _rmsnorm(x, g_ref[...]).astype(BF16)
    d_ff = wg_ref.shape[1]
    acc = x
    for c0 in range(0, d_ff, ff_chunk):
        gate = jnp.dot(hn, wg_ref[:, c0:c0 + ff_chunk], preferred_element_type=F32)
        up = jnp.dot(hn, wu_ref[:, c0:c0 + ff_chunk], preferred_element_type=F32)
        h = (gate * _sigmoid(gate) * up).astype(BF16)
        acc = acc + jnp.dot(h, wd_ref[c0:c0 + ff_chunk, :], preferred_element_type=F32)
    if final_norm:
        acc = _rmsnorm(acc, fg_ref[...])
    o_ref[...] = acc


def _ffn(x2, g, wg, wu, wd, final_g, *, final_norm, tm=512, ff_chunk=256):
    t, d = x2.shape
    d_ff = wg.shape[1]
    return pl.pallas_call(
        functools.partial(_ffn_kernel, ff_chunk=ff_chunk, final_norm=final_norm),
        grid=(t // tm,),
        in_specs=[pl.BlockSpec((tm, d), lambda i: (i, 0)), _resident((1, d)),
                  _resident((d, d_ff)), _resident((d, d_ff)), _resident((d_ff, d)),
                  _resident((1, d))],
        out_specs=pl.BlockSpec((tm, d), lambda i: (i, 0)),
        out_shape=jax.ShapeDtypeStruct((t, d), F32),
        compiler_params=_params("parallel"),
        name="ffn",
    )(x2, g.reshape(1, d), wg, wu, wd, final_g.reshape(1, d))


def _gate_scan_kernel(z_ref, bias_ref, tri_ref, o_ref, carry_ref, *, n_forget, carry):
    j = pl.program_id(1)

    @pl.when(j == 0)
    def _():
        carry_ref[...] = jnp.zeros_like(carry_ref)

    z = z_ref[0] + bias_ref[...]
    lane = lax.broadcasted_iota(jnp.int32, z.shape, 1)
    forget = lane < n_forget
    lf = jnp.where(forget, _log_sigmoid(z), 0.0)
    cum = jnp.dot(tri_ref[...], lf, preferred_element_type=F32,
                  precision=lax.Precision.HIGHEST)
    if carry:
        cum = cum + carry_ref[...]
        carry_ref[...] = cum[C_SPAN - 1:C_SPAN, :]
    o_ref[0] = jnp.where(forget, cum, z)


def _gate_scan(z, bias, tri, *, n_forget, carry):
    b, s, _ = z.shape
    rows = tri.shape[0]
    return pl.pallas_call(
        functools.partial(_gate_scan_kernel, n_forget=n_forget, carry=carry),
        grid=(b, s // rows),
        in_specs=[pl.BlockSpec((1, rows, LANES), lambda i, j: (i, j, 0)),
                  _resident((1, LANES)), _resident((rows, rows))],
        out_specs=pl.BlockSpec((1, rows, LANES), lambda i, j: (i, j, 0)),
        out_shape=jax.ShapeDtypeStruct((b, s, LANES), F32),
        scratch_shapes=[pltpu.VMEM((1, LANES), F32)],
        compiler_params=_params("parallel", "arbitrary"),
        name="gate_scan",
    )(z, bias, tri)


def _fox_kernel(q_ref, k_ref, v_ref, ccol_ref, crow_ref, o_ref, *, tq, tk):
    i = pl.program_id(2)
    lane = lax.broadcasted_iota(jnp.int32, (tq, LANES), 1)
    q = q_ref[0]
    q = q * jnp.asarray(A_HEAD_DIM ** -0.5, q.dtype)
    outs = []
    for h in range(2):
        in_head = (lane >= h * A_HEAD_DIM) & (lane < (h + 1) * A_HEAD_DIM)
        qh = jnp.where(in_head, q, jnp.zeros_like(q))
        cq = ccol_ref[0, 0, :, h:h + 1]

        def scores(j):
            k0 = pl.multiple_of(j * tk, tk)
            kb = k_ref[0, pl.ds(k0, tk), :]
            s = lax.dot_general(qh, kb, (((1,), (1,)), ((), ())), preferred_element_type=F32)
            ck = crow_ref[0, 0, h:h + 1, pl.ds(k0, tk)]
            return s + (cq - ck), k0

        def update(s, k0, m, l, acc):
            m_new = jnp.maximum(m, jnp.max(s, axis=-1, keepdims=True))
            alpha = jnp.exp(m - m_new)
            p = jnp.exp(s - m_new)
            l = alpha * l + jnp.sum(p, axis=-1, keepdims=True)
            vb = v_ref[0, pl.ds(k0, tk), :]
            acc = alpha * acc + jnp.dot(p.astype(BF16), vb, preferred_element_type=F32)
            return m_new, l, acc

        def body(j, carry):
            s, k0 = scores(j)
            return update(s, k0, *carry)

        init = (jnp.full((tq, 1), -jnp.inf, F32), jnp.zeros((tq, 1), F32),
                jnp.zeros((tq, LANES), F32))
        m, l, acc = lax.fori_loop(0, i, body, init)
        s, k0 = scores(i)
        row = lax.broadcasted_iota(jnp.int32, (tq, tk), 0)
        col = lax.broadcasted_iota(jnp.int32, (tq, tk), 1)
        s = jnp.where(row >= col, s, -jnp.inf)
        m, l, acc = update(s, k0, m, l, acc)
        outs.append(acc / l)
    o_ref[0] = jnp.where(lane < A_HEAD_DIM, outs[0], outs[1]).astype(o_ref.dtype)


def _fox_attention(qkv, ccol, crow, *, tq=256):
    b, s, d3 = qkv.shape
    d = d3 // 3
    pairs = d // LANES
    return pl.pallas_call(
        functools.partial(_fox_kernel, tq=tq, tk=tq),
        grid=(b, pairs, s // tq),
        in_specs=[pl.BlockSpec((1, tq, LANES), lambda bi, p, i: (bi, i, p)),
                  pl.BlockSpec((1, s, LANES), lambda bi, p, i: (bi, 0, pairs + p)),
                  pl.BlockSpec((1, s, LANES), lambda bi, p, i: (bi, 0, 2 * pairs + p)),
                  pl.BlockSpec((1, 1, tq, 2), lambda bi, p, i: (bi, p, i, 0)),
                  pl.BlockSpec((1, 1, 2, s), lambda bi, p, i: (bi, p, 0, 0))],
        out_specs=pl.BlockSpec((1, tq, LANES), lambda bi, p, i: (bi, i, p)),
        out_shape=jax.ShapeDtypeStruct((b, s, d), BF16),
        compiler_params=_params("parallel", "parallel", "arbitrary"),
        name="fox_attention",
    )(qkv, qkv, qkv, ccol, crow)


def _mlstm_kernel(q_ref, k_ref, v_ref, o_ref, gcol_ref, grow_ref, cw_ref, cb_ref, ng_ref,
                  out_ref, c_ref, m_ref, tailq_ref, tailk_ref, *, chunk):
    L = chunk
    d = B_HEAD_DIM
    width = B_HEADS * d

    @pl.when(pl.program_id(1) == 0)
    def _():
        c_ref[...] = jnp.zeros_like(c_ref)
        m_ref[...] = jnp.zeros_like(m_ref)
        tailq_ref[...] = jnp.zeros_like(tailq_ref)
        tailk_ref[...] = jnp.zeros_like(tailk_ref)

    def conv_silu(x_ref, tail_ref, col0):
        x = x_ref[0].astype(F32)
        ext = jnp.concatenate([tail_ref[...], x], axis=0)
        y = cb_ref[:, col0:col0 + width]
        for t in range(B_CONV):
            off = 8 - (B_CONV - 1) + t
            y = y + cw_ref[t:t + 1, col0:col0 + width] * ext[off:off + L]
        tail_ref[...] = x[L - 8:L]
        return y * _sigmoid(y)

    q = conv_silu(q_ref, tailq_ref, 0)
    k = conv_silu(k_ref, tailk_ref, width) * (d ** -0.5)
    v = v_ref[0]
    gcol = gcol_ref[0]
    grow = grow_ref[0, 0]
    row = lax.broadcasted_iota(jnp.int32, (L, L), 0)
    col = lax.broadcasted_iota(jnp.int32, (L, L), 1)
    tri = row >= col
    lane = lax.broadcasted_iota(jnp.int32, (L, d), 1)
    ones_col = jnp.where(lane == 0, 1.0, 0.0).astype(BF16)

    for h in range(B_HEADS):
        sl = slice(h * d, (h + 1) * d)
        qh = q[:, sl].astype(BF16)
        kf = k[:, sl]
        kh = kf.astype(BF16)
        vaug = jnp.concatenate([v[:, sl], ones_col], axis=1)
        bc = gcol[:, h:h + 1]
        igc = gcol[:, B_HEADS + h:B_HEADS + h + 1]
        br = grow[h:h + 1, :]
        igr = grow[B_HEADS + h:B_HEADS + h + 1, :]
        m_old = m_ref[h:h + 1, 0:1]

        log_d = jnp.where(tri, bc - br + igr, -jnp.inf)
        m_inter = bc + m_old
        m_t = jnp.maximum(m_inter, jnp.max(log_d, axis=-1, keepdims=True))
        dm = jnp.exp(log_d - m_t)
        a = jnp.exp(m_inter - m_t)
        sqk = lax.dot_general(qh, kh, (((1,), (1,)), ((), ())), preferred_element_type=F32) * dm
        c_aug = c_ref[h]
        num_aug = (a * jnp.dot(qh, c_aug.astype(BF16), preferred_element_type=F32)
                   + jnp.dot(sqk.astype(BF16), vaug, preferred_element_type=F32))
        num = num_aug[:, :d]
        den = num_aug[:, d:d + 1]
        hc = num / jnp.maximum(jnp.abs(den), jnp.exp(-m_t))
        hc = hc * lax.rsqrt(jnp.mean(hc * hc, axis=-1, keepdims=True) + EPS)
        gate = _sigmoid(o_ref[0, :, sl].astype(F32))
        out_ref[0, :, sl] = (hc * ng_ref[:, sl] * gate).astype(out_ref.dtype)

        b_last = bc[L - 1:L, :]
        g = b_last - bc + igc
        m_new = jnp.maximum(b_last + m_old, jnp.max(g, axis=0, keepdims=True))
        w = jnp.exp(g - m_new)
        decay = jnp.exp(b_last + m_old - m_new)
        kw = (kf * w).astype(BF16)
        upd = lax.dot_general(kw, vaug, (((0,), (0,)), ((), ())), preferred_element_type=F32)
        c_ref[h] = decay * c_aug + upd
        m_ref[h:h + 1, :] = jnp.broadcast_to(m_new, (1, LANES))


def _mlstm(proj, gcol, grow, conv_w, conv_b, norm_g, *, chunk):
    b, s, d4 = proj.shape
    d = d4 // 4
    nc = s // chunk
    blk = lambda c0: pl.BlockSpec((1, chunk, d), lambda bi, c: (bi, c, c0))
    return pl.pallas_call(
        functools.partial(_mlstm_kernel, chunk=chunk),
        grid=(b, nc),
        in_specs=[blk(0), blk(1), blk(2), blk(3),
                  pl.BlockSpec((1, chunk, LANES), lambda bi, c: (bi, c, 0)),
                  pl.BlockSpec((1, 1, 2 * B_HEADS, chunk), lambda bi, c: (bi, c, 0, 0)),
                  _resident((B_CONV, 2 * d)), _resident((1, 2 * d)), _resident((1, d))],
        out_specs=pl.BlockSpec((1, chunk, d), lambda bi, c: (bi, c, 0)),
        out_shape=jax.ShapeDtypeStruct((b, s, d), BF16),
        scratch_shapes=[pltpu.VMEM((B_HEADS, B_HEAD_DIM, 2 * B_HEAD_DIM), F32),
                        pltpu.VMEM((B_HEADS, LANES), F32),
                        pltpu.VMEM((8, d), F32), pltpu.VMEM((8, d), F32)],
        compiler_params=_params("parallel", "arbitrary"),
        name="mlstm",
    )(proj, proj, proj, proj, gcol, grow, conv_w, conv_b.reshape(1, 2 * d), norm_g.reshape(1, d))


def _gmlp_kernel(x_ref, g_ref, wu_ref, wv_ref, lng_ref, lnb_ref, ws_ref, bs_ref, o_ref, v_ref):
    tm = x_ref.shape[0]
    width = wv_ref.shape[1]
    gw = width // C_GROUPS
    hn = _rmsnorm(x_ref[...], g_ref[...]).astype(BF16)
    for c0 in range(0, width, gw):
        v_ref[:, c0:c0 + gw] = _gelu_exact(
            jnp.dot(hn, wv_ref[:, c0:c0 + gw], preferred_element_type=F32))
    v = v_ref[...]
    mu = jnp.mean(v, axis=-1, keepdims=True)
    vc = v - mu
    var = jnp.mean(vc * vc, axis=-1, keepdims=True)
    vn = (vc * lax.rsqrt(var + EPS) * lng_ref[...] + lnb_ref[...]).astype(BF16)
    row = lax.broadcasted_iota(jnp.int32, (C_SPAN, C_SPAN), 0)
    col = lax.broadcasted_iota(jnp.int32, (C_SPAN, C_SPAN), 1)
    mask = (row // CHUNK) >= (col // CHUNK)
    for gi in range(C_GROUPS):
        cs = slice(gi * gw, (gi + 1) * gw)
        ws = jnp.where(mask, ws_ref[gi], 0.0).astype(BF16)
        bias = bs_ref[:, gi:gi + 1]
        u = _gelu_exact(jnp.dot(hn, wu_ref[:, cs], preferred_element_type=F32))
        for r0 in range(0, tm, C_SPAN):
            s = jnp.dot(ws, vn[r0:r0 + C_SPAN, cs], preferred_element_type=F32) + bias
            o_ref[r0:r0 + C_SPAN, cs] = (u[r0:r0 + C_SPAN] * s).astype(o_ref.dtype)


def _gmlp(x2, g, wu, wv, ln_g, ln_b, w_s, bs_t, *, tm=256):
    t, d = x2.shape
    width = wu.shape[1]
    return pl.pallas_call(
        _gmlp_kernel,
        grid=(t // tm,),
        in_specs=[pl.BlockSpec((tm, d), lambda i: (i, 0)), _resident((1, d)),
                  _resident((d, width)), _resident((d, width)),
                  _resident((1, width)), _resident((1, width)),
                  _resident((C_GROUPS, C_SPAN, C_SPAN)), _resident((C_SPAN, LANES))],
        out_specs=pl.BlockSpec((tm, width), lambda i: (i, 0)),
        out_shape=jax.ShapeDtypeStruct((t, width), BF16),
        scratch_shapes=[pltpu.VMEM((tm, width), F32)],
        compiler_params=_params("parallel"),
        name="gmlp",
    )(x2, g.reshape(1, d), wu, wv, ln_g.reshape(1, width), ln_b.reshape(1, width), w_s, bs_t)


def _pad_lanes(a):
    return jnp.pad(a, [(0, 0)] * (a.ndim - 1) + [(0, LANES - a.shape[-1])])


def _tri(rows, seg):
    r = jnp.arange(rows)
    return ((r[:, None] >= r[None, :]) & (r[:, None] // seg == r[None, :] // seg)).astype(F32)


def _mixer_a(x2, g, w_in, b_f, w_out, b, s):
    d = x2.shape[1]
    qkv, zf = _norm_proj(x2, g, w_in[:, :3 * d].astype(BF16), _pad_lanes(w_in[:, 3 * d:]).astype(BF16))
    c = _gate_scan(zf.reshape(b, s, LANES), _pad_lanes(b_f[None, :]), _tri(C_SPAN, C_SPAN),
                   n_forget=A_HEADS, carry=True)[..., :A_HEADS]
    ccol = c.reshape(b, s, A_HEADS // 2, 2).transpose(0, 2, 1, 3)
    crow = c.reshape(b, s, A_HEADS // 2, 2).transpose(0, 2, 3, 1)
    o = _fox_attention(qkv.reshape(b, s, 3 * d), ccol, crow)
    return _matmul_res(o.reshape(b * s, d), w_out.astype(BF16), x2)


def _mixer_b(x2, g, w_in, conv_w, conv_b, b_i, b_f, norm_g, w_out, b, s):
    d = x2.shape[1]
    chunk = MLSTM_CHUNK
    w_gate = jnp.concatenate([w_in[:, 4 * d + B_HEADS:], w_in[:, 4 * d:4 * d + B_HEADS]], axis=1)
    proj, zg = _norm_proj(x2, g, w_in[:, :4 * d].astype(BF16), _pad_lanes(w_gate).astype(BF16))
    bias = _pad_lanes(jnp.concatenate([b_f, b_i])[None, :])
    gcol = _gate_scan(zg.reshape(b, s, LANES), bias, _tri(C_SPAN, chunk),
                      n_forget=B_HEADS, carry=False)
    grow = gcol[..., :2 * B_HEADS].reshape(b, s // chunk, chunk, 2 * B_HEADS).transpose(0, 1, 3, 2)
    hs = _mlstm(proj.reshape(b, s, 4 * d), gcol, grow, conv_w, conv_b, norm_g, chunk=chunk)
    return _matmul_res(hs.reshape(b * s, d), w_out.astype(BF16), x2)


def _mixer_c(x2, g, w_in, ln_g, ln_b, w_s, b_s, w_out):
    width = w_in.shape[1] // 2
    gated = _gmlp(x2, g, w_in[:, :width].astype(BF16), w_in[:, width:].astype(BF16),
                  ln_g, ln_b, w_s, _pad_lanes(b_s.T))
    return _matmul_res(gated, w_out.astype(BF16), x2)


def kernel(x, norm1_g, norm2_g, final_g, a_w_in, a_b_f, a_w_out, b_w_in, b_conv_w, b_conv_b,
           b_b_i, b_b_f, b_norm_g, b_w_out, c_w_in, c_ln_g, c_ln_b, c_w_s, c_b_s, c_w_out,
           ffn_w_gu, ffn_w_down):
    b, s, d = x.shape
    depth = norm1_g.shape[0]
    d_ff = ffn_w_down.shape[1]
    x2 = x.reshape(b * s, d)
    for i in range(depth):
        kind, j = i % 3, i // 3
        if kind == 0:
            x2 = _mixer_a(x2, norm1_g[i], a_w_in[j], a_b_f[j], a_w_out[j], b, s)
        elif kind == 1:
            x2 = _mixer_b(x2, norm1_g[i], b_w_in[j], b_conv_w[j], b_conv_b[j], b_b_i[j],
                          b_b_f[j], b_norm_g[j], b_w_out[j], b, s)
        else:
            x2 = _mixer_c(x2, norm1_g[i], c_w_in[j], c_ln_g[j], c_ln_b[j], c_w_s[j], c_b_s[j],
                          c_w_out[j])
        d_ff = ffn_w_gu.shape[2] // 2
        x2 = _ffn(x2, norm2_g[i], ffn_w_gu[i, :, :d_ff].astype(BF16),
                  ffn_w_gu[i, :, d_ff:].astype(BF16), ffn_w_down[i].astype(BF16), final_g,
                  final_norm=(i == depth - 1))
    return x2.reshape(b, s, d)
```

```python
import functools
import math

import jax
import jax.numpy as jnp
from jax import lax
from jax.experimental import pallas as pl
from jax.experimental.pallas import tpu as pltpu

F32 = jnp.float32
BF16 = jnp.bfloat16

EPS = 1e-6
LOG2E = math.log2(math.e)
LANES = 128
A_HEADS = 16
A_HEAD_DIM = 64
B_HEADS = 8
B_HEAD_DIM = 128
B_CONV = 4
C_GROUPS = 8
C_SPAN = 128
CHUNK = 64
MLSTM_CHUNK = 64
VMEM_LIMIT = 56 * 1024 * 1024


def _params(*sem):
    return pltpu.CompilerParams(dimension_semantics=sem, vmem_limit_bytes=VMEM_LIMIT)


def _resident(shape):
    nd = len(shape)
    return pl.BlockSpec(shape, lambda *_: (0,) * nd, pipeline_mode=pl.Buffered(1))


def _rmsnorm(x, g):
    ms = jnp.mean(x * x, axis=-1, keepdims=True)
    return x * lax.rsqrt(ms + EPS) * g


def _sigmoid(x):
    return 1.0 / (1.0 + jnp.exp(-x))


def _log_sigmoid(z):
    return jnp.minimum(z, 0.0) - jnp.log1p(jnp.exp(-jnp.abs(z)))


def _gelu_exact(x):
    return 0.5 * x * (1.0 + lax.erf(x * (1.0 / math.sqrt(2.0))))


def _norm_proj_kernel(x_ref, g_ref, w_ref, wg_ref, o_ref, og_ref, *, col_chunk, lead_cols,
                      lead_scale):
    hn = _rmsnorm(x_ref[...], g_ref[...]).astype(BF16)
    n = w_ref.shape[1]
    for c0 in range(0, n, col_chunk):
        y = jnp.dot(hn, w_ref[:, c0:c0 + col_chunk], preferred_element_type=F32)
        if c0 + col_chunk <= lead_cols:
            y = y * lead_scale
        o_ref[:, c0:c0 + col_chunk] = y.astype(o_ref.dtype)
    og_ref[...] = jnp.dot(hn, wg_ref[...], preferred_element_type=F32)


def _norm_proj(x2, g, w, wg, *, tm=512, col_chunk=512, lead_cols=0, lead_scale=1.0):
    t, d = x2.shape
    n = w.shape[1]
    assert lead_cols % col_chunk == 0
    return pl.pallas_call(
        functools.partial(_norm_proj_kernel, col_chunk=col_chunk, lead_cols=lead_cols,
                          lead_scale=lead_scale),
        grid=(t // tm,),
        in_specs=[pl.BlockSpec((tm, d), lambda i: (i, 0)),
                  _resident((1, d)), _resident((d, n)), _resident((d, LANES))],
        out_specs=[pl.BlockSpec((tm, n), lambda i: (i, 0)),
                   pl.BlockSpec((tm, LANES), lambda i: (i, 0))],
        out_shape=[jax.ShapeDtypeStruct((t, n), BF16),
                   jax.ShapeDtypeStruct((t, LANES), F32)],
        compiler_params=_params("parallel"),
        name="norm_proj",
    )(x2, g.reshape(1, d), w, wg)


def _matmul_res_kernel(a_ref, w_ref, x_ref, o_ref):
    o_ref[...] = x_ref[...] + jnp.dot(a_ref[...], w_ref[...], preferred_element_type=F32)


def _matmul_res(a, w, x2, *, tm=512):
    t, k = a.shape
    d = w.shape[1]
    return pl.pallas_call(
        _matmul_res_kernel,
        grid=(t // tm,),
        in_specs=[pl.BlockSpec((tm, k), lambda i: (i, 0)), _resident((k, d)),
                  pl.BlockSpec((tm, d), lambda i: (i, 0))],
        out_specs=pl.BlockSpec((tm, d), lambda i: (i, 0)),
        out_shape=jax.ShapeDtypeStruct((t, d), F32),
        compiler_params=_params("parallel"),
        name="matmul_res",
    )(a, w, x2)


def _ffn_kernel(x_ref, g_ref, wg_ref, wu_ref, wd_ref, fg_ref, o_ref, *, ff_chunk, final_norm):
    x = x_ref[...]
    hn = _rmsnorm(x, g_ref[...]).astype(BF16)
    d_ff = wg_ref.shape[1]
    acc = x
    for c0 in range(0, d_ff, ff_chunk):
        gate = jnp.dot(hn, wg_ref[:, c0:c0 + ff_chunk], preferred_element_type=F32)
        up = jnp.dot(hn, wu_ref[:, c0:c0 + ff_chunk], preferred_element_type=F32)
        h = (gate * _sigmoid(gate) * up).astype(BF16)
        acc = acc + jnp.dot(h, wd_ref[c0:c0 + ff_chunk, :], preferred_element_type=F32)
    if final_norm:
        acc = _rmsnorm(acc, fg_ref[...])
    o_ref[...] = acc


def _ffn(x2, g, wg, wu, wd, final_g, *, final_norm, tm=512, ff_chunk=256):
    t, d = x2.shape
    d_ff = wg.shape[1]
    return pl.pallas_call(
        functools.partial(_ffn_kernel, ff_chunk=ff_chunk, final_norm=final_norm),
        grid=(t // tm,),
        in_specs=[pl.BlockSpec((tm, d), lambda i: (i, 0)), _resident((1, d)),
                  _resident((d, d_ff)), _resident((d, d_ff)), _resident((d_ff, d)),
                  _resident((1, d))],
        out_specs=pl.BlockSpec((tm, d), lambda i: (i, 0)),
        out_shape=jax.ShapeDtypeStruct((t, d), F32),
        compiler_params=_params("parallel"),
        name="ffn",
    )(x2, g.reshape(1, d), wg, wu, wd, final_g.reshape(1, d))


def _gate_scan_kernel(z_ref, bias_ref, tri_ref, o_ref, *, n_forget, carry):
    rows = tri_ref.shape[0]
    lane = lax.broadcasted_iota(jnp.int32, (rows, LANES), 1)
    forget = lane < n_forget
    tri = tri_ref[...]
    run = jnp.zeros((1, LANES), F32)
    for r0 in range(0, z_ref.shape[1], rows):
        z = z_ref[0, r0:r0 + rows] + bias_ref[...]
        lf = jnp.where(forget, _log_sigmoid(z), 0.0)
        cum = jnp.dot(tri, lf, preferred_element_type=F32, precision=lax.Precision.HIGHEST)
        if carry:
            cum = cum + run
            run = cum[rows - 1:rows, :]
        o_ref[0, r0:r0 + rows] = jnp.where(forget, cum, z)


def _gate_scan(z, bias, tri, *, n_forget, carry):
    b, s, _ = z.shape
    rows = tri.shape[0]
    return pl.pallas_call(
        functools.partial(_gate_scan_kernel, n_forget=n_forget, carry=carry),
        grid=(b,),
        in_specs=[pl.BlockSpec((1, s, LANES), lambda i: (i, 0, 0)),
                  _resident((1, LANES)), _resident((rows, rows))],
        out_specs=pl.BlockSpec((1, s, LANES), lambda i: (i, 0, 0)),
        out_shape=jax.ShapeDtypeStruct((b, s, LANES), F32),
        compiler_params=_params("parallel"),
        name="gate_scan",
    )(z, bias, tri)


def _fox_kernel(q_ref, k_ref, v_ref, crow_ref, o_ref, vaug_ref, ck_ref, acc_ref, *, tq, tk, td):
    i = pl.program_id(2)
    s_len = k_ref.shape[1]
    hd = A_HEAD_DIM
    ones_lane = (hd, 0)

    @pl.when(i == 0)
    def _():
        lane_s = lax.broadcasted_iota(jnp.int32, (s_len, LANES), 1)
        v = v_ref[0]
        for h in range(2):
            vaug_ref[h] = jnp.where(lane_s == ones_lane[h], jnp.ones_like(v), v)
        ck_ref[...] = crow_ref[0, 0] * LOG2E

    lane = lax.broadcasted_iota(jnp.int32, (tq, LANES), 1)
    q = q_ref[0]
    qh = (jnp.where(lane < hd, q, jnp.zeros_like(q)), jnp.where(lane >= hd, q, jnp.zeros_like(q)))
    acc_ref[...] = jnp.zeros_like(acc_ref)

    def step(ms, r0, k0, ncols, masked):
        kb = k_ref[0, pl.ds(k0, ncols), :]
        out = []
        for h in range(2):
            s = lax.dot_general(qh[h][r0:], kb, (((1,), (1,)), ((), ())),
                                preferred_element_type=F32)
            s = s - ck_ref[h:h + 1, pl.ds(k0, ncols)]
            if masked:
                row = lax.broadcasted_iota(jnp.int32, (tq - r0, ncols), 0)
                col = lax.broadcasted_iota(jnp.int32, (tq - r0, ncols), 1)
                s = jnp.where(row >= col, s, -jnp.inf)
            m_old = ms[h][r0:]
            m_new = jnp.maximum(m_old, jnp.max(s, axis=-1, keepdims=True))
            alpha = jnp.exp2(m_old - m_new)
            p = jnp.exp2(s - m_new).astype(BF16)
            acc_ref[h, r0:] = alpha * acc_ref[h, r0:] + jnp.dot(
                p, vaug_ref[h, pl.ds(k0, ncols), :], preferred_element_type=F32)
            out.append(m_new if r0 == 0 else jnp.concatenate([ms[h][:r0], m_new], axis=0))
        return tuple(out)

    ms = (jnp.full((tq, 1), -jnp.inf, F32),) * 2
    ms = lax.fori_loop(0, i * (tq // tk),
                       lambda j, c: step(c, 0, pl.multiple_of(j * tk, tk), tk, False), ms)
    for r0 in range(0, tq, td):
        ms = step(ms, r0, pl.multiple_of(i * tq + r0, td), td, True)
    a0 = acc_ref[0]
    a1 = acc_ref[1]
    o_ref[0] = jnp.where(lane < hd, a0 / a0[:, ones_lane[0]:ones_lane[0] + 1],
                         a1 / a1[:, ones_lane[1]:ones_lane[1] + 1]).astype(o_ref.dtype)


def _fox_attention(qkv, crow, *, tq=512, tk=512, td=256):
    b, s, d3 = qkv.shape
    d = d3 // 3
    pairs = d // LANES
    return pl.pallas_call(
        functools.partial(_fox_kernel, tq=tq, tk=tk, td=td),
        grid=(b, pairs, s // tq),
        in_specs=[pl.BlockSpec((1, tq, LANES), lambda bi, p, i: (bi, i, p)),
                  pl.BlockSpec((1, s, LANES), lambda bi, p, i: (bi, 0, pairs + p)),
                  pl.BlockSpec((1, s, LANES), lambda bi, p, i: (bi, 0, 2 * pairs + p)),
                  pl.BlockSpec((1, 1, 2, s), lambda bi, p, i: (bi, p, 0, 0))],
        out_specs=pl.BlockSpec((1, tq, LANES), lambda bi, p, i: (bi, i, p)),
        out_shape=jax.ShapeDtypeStruct((b, s, d), BF16),
        scratch_shapes=[pltpu.VMEM((2, s, LANES), BF16), pltpu.VMEM((2, s), F32),
                        pltpu.VMEM((2, tq, LANES), F32)],
        compiler_params=_params("parallel", "parallel", "arbitrary"),
        name="fox_attention",
    )(qkv, qkv, qkv, crow)


def _mlstm_kernel(q_ref, k_ref, v_ref, o_ref, gcol_ref, grow_ref, cw_ref, cb_ref, ng_ref,
                  out_ref, c_ref, m_ref, tailq_ref, tailk_ref, *, chunk):
    L = chunk
    d = B_HEAD_DIM
    width = B_HEADS * d

    @pl.when(pl.program_id(1) == 0)
    def _():
        c_ref[...] = jnp.zeros_like(c_ref)
        m_ref[...] = jnp.zeros_like(m_ref)
        tailq_ref[...] = jnp.zeros_like(tailq_ref)
        tailk_ref[...] = jnp.zeros_like(tailk_ref)

    def conv_silu(x_ref, tail_ref, col0):
        x = x_ref[0].astype(F32)
        ext = jnp.concatenate([tail_ref[...], x], axis=0)
        y = cb_ref[:, col0:col0 + width]
        for t in range(B_CONV):
            off = 8 - (B_CONV - 1) + t
            y = y + cw_ref[t:t + 1, col0:col0 + width] * ext[off:off + L]
        tail_ref[...] = x[L - 8:L]
        return y * _sigmoid(y)

    q = conv_silu(q_ref, tailq_ref, 0)
    k = conv_silu(k_ref, tailk_ref, width) * (d ** -0.5)
    v = v_ref[0]
    gcol = gcol_ref[0]
    grow = grow_ref[0, 0]
    row = lax.broadcasted_iota(jnp.int32, (L, L), 0)
    col = lax.broadcasted_iota(jnp.int32, (L, L), 1)
    tri = row >= col
    lane = lax.broadcasted_iota(jnp.int32, (L, d), 1)
    ones_col = jnp.where(lane == 0, 1.0, 0.0).astype(BF16)

    for h in range(B_HEADS):
        sl = slice(h * d, (h + 1) * d)
        qh = q[:, sl].astype(BF16)
        kf = k[:, sl]
        kh = kf.astype(BF16)
        vaug = jnp.concatenate([v[:, sl], ones_col], axis=1)
        bc = gcol[:, h:h + 1]
        igc = gcol[:, B_HEADS + h:B_HEADS + h + 1]
        br = grow[h:h + 1, :]
        igr = grow[B_HEADS + h:B_HEADS + h + 1, :]
        m_old = m_ref[h:h + 1, 0:1]

        log_d = jnp.where(tri, bc - br + igr, -jnp.inf)
        m_inter = bc + m_old
        m_t = jnp.maximum(m_inter, jnp.max(log_d, axis=-1, keepdims=True))
        dm = jnp.exp(log_d - m_t)
        a = jnp.exp(m_inter - m_t)
        sqk = lax.dot_general(qh, kh, (((1,), (1,)), ((), ())), preferred_element_type=F32) * dm
        c_aug = c_ref[h]
        num_aug = (a * jnp.dot(qh, c_aug.astype(BF16), preferred_element_type=F32)
                   + jnp.dot(sqk.astype(BF16), vaug, preferred_element_type=F32))
        num = num_aug[:, :d]
        den = num_aug[:, d:d + 1]
        hc = num / jnp.maximum(jnp.abs(den), jnp.exp(-m_t))
        hc = hc * lax.rsqrt(jnp.mean(hc * hc, axis=-1, keepdims=True) + EPS)
        gate = _sigmoid(o_ref[0, :, sl].astype(F32))
        out_ref[0, :, sl] = (hc * ng_ref[:, sl] * gate).astype(out_ref.dtype)

        b_last = bc[L - 1:L, :]
        g = b_last - bc + igc
        m_new = jnp.maximum(b_last + m_old, jnp.max(g, axis=0, keepdims=True))
        w = jnp.exp(g - m_new)
        decay = jnp.exp(b_last + m_old - m_new)
        kw = (kf * w).astype(BF16)
        upd = lax.dot_general(kw, vaug, (((0,), (0,)), ((), ())), preferred_element_type=F32)
        c_ref[h] = decay * c_aug + upd
        m_ref[h:h + 1, :] = jnp.broadcast_to(m_new, (1, LANES))


def _mlstm(proj, gcol, grow, conv_w, conv_b, norm_g, *, chunk):
    b, s, d4 = proj.shape
    d = d4 // 4
    nc = s // chunk
    blk = lambda c0: pl.BlockSpec((1, chunk, d), lambda bi, c: (bi, c, c0))
    return pl.pallas_call(
        functools.partial(_mlstm_kernel, chunk=chunk),
        grid=(b, nc),
        in_specs=[blk(0), blk(1), blk(2), blk(3),
                  pl.BlockSpec((1, chunk, LANES), lambda bi, c: (bi, c, 0)),
                  pl.BlockSpec((1, 1, 2 * B_HEADS, chunk), lambda bi, c: (bi, c, 0, 0)),
                  _resident((B_CONV, 2 * d)), _resident((1, 2 * d)), _resident((1, d))],
        out_specs=pl.BlockSpec((1, chunk, d), lambda bi, c: (bi, c, 0)),
        out_shape=jax.ShapeDtypeStruct((b, s, d), BF16),
        scratch_shapes=[pltpu.VMEM((B_HEADS, B_HEAD_DIM, 2 * B_HEAD_DIM), F32),
                        pltpu.VMEM((B_HEADS, LANES), F32),
                        pltpu.VMEM((8, d), F32), pltpu.VMEM((8, d), F32)],
        compiler_params=_params("parallel", "arbitrary"),
        name="mlstm",
    )(proj, proj, proj, proj, gcol, grow, conv_w, conv_b.reshape(1, 2 * d), norm_g.reshape(1, d))


def _gmlp_kernel(x_ref, g_ref, wu_ref, wv_ref, lng_ref, lnb_ref, ws_ref, bs_ref, o_ref, v_ref):
    tm = x_ref.shape[0]
    width = wv_ref.shape[1]
    gw = width // C_GROUPS
    hn = _rmsnorm(x_ref[...], g_ref[...]).astype(BF16)
    for c0 in range(0, width, gw):
        v_ref[:, c0:c0 + gw] = _gelu_exact(
            jnp.dot(hn, wv_ref[:, c0:c0 + gw], preferred_element_type=F32))
    v = v_ref[...]
    mu = jnp.mean(v, axis=-1, keepdims=True)
    vc = v - mu
    var = jnp.mean(vc * vc, axis=-1, keepdims=True)
    vn = (vc * lax.rsqrt(var + EPS) * lng_ref[...] + lnb_ref[...]).astype(BF16)
    row = lax.broadcasted_iota(jnp.int32, (C_SPAN, C_SPAN), 0)
    col = lax.broadcasted_iota(jnp.int32, (C_SPAN, C_SPAN), 1)
    mask = (row // CHUNK) >= (col // CHUNK)
    for gi in range(C_GROUPS):
        cs = slice(gi * gw, (gi + 1) * gw)
        ws = jnp.where(mask, ws_ref[gi], 0.0).astype(BF16)
        bias = bs_ref[:, gi:gi + 1]
        u = _gelu_exact(jnp.dot(hn, wu_ref[:, cs], preferred_element_type=F32))
        for r0 in range(0, tm, C_SPAN):
            s = jnp.dot(ws, vn[r0:r0 + C_SPAN, cs], preferred_element_type=F32) + bias
            o_ref[r0:r0 + C_SPAN, cs] = (u[r0:r0 + C_SPAN] * s).astype(o_ref.dtype)


def _gmlp(x2, g, wu, wv, ln_g, ln_b, w_s, bs_t, *, tm=256):
    t, d = x2.shape
    width = wu.shape[1]
    return pl.pallas_call(
        _gmlp_kernel,
        grid=(t // tm,),
        in_specs=[pl.BlockSpec((tm, d), lambda i: (i, 0)), _resident((1, d)),
                  _resident((d, width)), _resident((d, width)),
                  _resident((1, width)), _resident((1, width)),
                  _resident((C_GROUPS, C_SPAN, C_SPAN)), _resident((C_SPAN, LANES))],
        out_specs=pl.BlockSpec((tm, width), lambda i: (i, 0)),
        out_shape=jax.ShapeDtypeStruct((t, width), BF16),
        scratch_shapes=[pltpu.VMEM((tm, width), F32)],
        compiler_params=_params("parallel"),
        name="gmlp",
    )(x2, g.reshape(1, d), wu, wv, ln_g.reshape(1, width), ln_b.reshape(1, width), w_s, bs_t)


def _pad_lanes(a):
    return jnp.pad(a, [(0, 0)] * (a.ndim - 1) + [(0, LANES - a.shape[-1])])


def _tri(rows, seg):
    r = jnp.arange(rows)
    return ((r[:, None] >= r[None, :]) & (r[:, None] // seg == r[None, :] // seg)).astype(F32)


def _mixer_a(x2, g, w_in, b_f, w_out, b, s):
    d = x2.shape[1]
    qkv, zf = _norm_proj(x2, g, w_in[:, :3 * d].astype(BF16), _pad_lanes(w_in[:, 3 * d:]).astype(BF16),
                         lead_cols=d, lead_scale=A_HEAD_DIM ** -0.5 * LOG2E)
    c = _gate_scan(zf.reshape(b, s, LANES), _pad_lanes(b_f[None, :]), _tri(C_SPAN, C_SPAN),
                   n_forget=A_HEADS, carry=True)[..., :A_HEADS]
    crow = c.reshape(b, s, A_HEADS // 2, 2).transpose(0, 2, 3, 1)
    o = _fox_attention(qkv.reshape(b, s, 3 * d), crow)
    return _matmul_res(o.reshape(b * s, d), w_out.astype(BF16), x2)


def _mixer_b(x2, g, w_in, conv_w, conv_b, b_i, b_f, norm_g, w_out, b, s):
    d = x2.shape[1]
    chunk = MLSTM_CHUNK
    w_gate = jnp.concatenate([w_in[:, 4 * d + B_HEADS:], w_in[:, 4 * d:4 * d + B_HEADS]], axis=1)
    proj, zg = _norm_proj(x2, g, w_in[:, :4 * d].astype(BF16), _pad_lanes(w_gate).astype(BF16))
    bias = _pad_lanes(jnp.concatenate([b_f, b_i])[None, :])
    gcol = _gate_scan(zg.reshape(b, s, LANES), bias, _tri(C_SPAN, chunk),
                      n_forget=B_HEADS, carry=False)
    grow = gcol[..., :2 * B_HEADS].reshape(b, s // chunk, chunk, 2 * B_HEADS).transpose(0, 1, 3, 2)
    hs = _mlstm(proj.reshape(b, s, 4 * d), gcol, grow, conv_w, conv_b, norm_g, chunk=chunk)
    return _matmul_res(hs.reshape(b * s, d), w_out.astype(BF16), x2)


def _mixer_c(x2, g, w_in, ln_g, ln_b, w_s, b_s, w_out):
    width = w_in.shape[1] // 2
    gated = _gmlp(x2, g, w_in[:, :width].astype(BF16), w_in[:, width:].astype(BF16),
                  ln_g, ln_b, w_s, _pad_lanes(b_s.T))
    return _matmul_res(gated, w_out.astype(BF16), x2)


def kernel(x, norm1_g, norm2_g, final_g, a_w_in, a_b_f, a_w_out, b_w_in, b_conv_w, b_conv_b,
           b_b_i, b_b_f, b_norm_g, b_w_out, c_w_in, c_ln_g, c_ln_b, c_w_s, c_b_s, c_w_out,
           ffn_w_gu, ffn_w_down):
    b, s, d = x.shape
    depth = norm1_g.shape[0]
    d_ff = ffn_w_down.shape[1]
    x2 = x.reshape(b * s, d)
    for i in range(depth):
        kind, j = i % 3, i // 3
        if kind == 0:
            x2 = _mixer_a(x2, norm1_g[i], a_w_in[j], a_b_f[j], a_w_out[j], b, s)
        elif kind == 1:
            x2 = _mixer_b(x2, norm1_g[i], b_w_in[j], b_conv_w[j], b_conv_b[j], b_b_i[j],
                          b_b_f[j], b_norm_g[j], b_w_out[j], b, s)
        else:
            x2 = _mixer_c(x2, norm1_g[i], c_w_in[j], c_ln_g[j], c_ln_b[j], c_w_s[j], c_b_s[j],
                          c_w_out[j])
        d_ff = ffn_w_gu.shape[2] // 2
        x2 = _ffn(x2, norm2_g[i], ffn_w_gu[i, :, :d_ff].astype(BF16),
                  ffn_w_gu[i, :, d_ff:].astype(BF16), ffn_w_down[i].astype(BF16), final_g,
                  final_norm=(i == depth - 1))
    return x2.reshape(b, s, d)
```

```python
import functools
import math

import jax
import jax.numpy as jnp
from jax import lax
from jax.experimental import pallas as pl
from jax.experimental.pallas import tpu as pltpu

F32 = jnp.float32
BF16 = jnp.bfloat16

EPS = 1e-6
LOG2E = math.log2(math.e)
LANES = 128
A_HEADS = 16
A_HEAD_DIM = 64
B_HEADS = 8
B_HEAD_DIM = 128
B_CONV = 4
C_GROUPS = 8
C_SPAN = 128
CHUNK = 64
MLSTM_CHUNK = 256
VMEM_LIMIT = 56 * 1024 * 1024


def _params(*sem):
    return pltpu.CompilerParams(dimension_semantics=sem, vmem_limit_bytes=VMEM_LIMIT)


def _resident(shape):
    nd = len(shape)
    return pl.BlockSpec(shape, lambda *_: (0,) * nd, pipeline_mode=pl.Buffered(1))


def _layer_block(shape, layer, col_block=0):
    return pl.BlockSpec((None,) + tuple(shape), lambda *_: (layer, 0, col_block),
                        pipeline_mode=pl.Buffered(1))


def _rmsnorm(x, g):
    ms = jnp.mean(x * x, axis=-1, keepdims=True)
    return x * lax.rsqrt(ms + EPS) * g


def _sigmoid(x):
    return 0.5 + 0.5 * jnp.tanh(0.5 * x)


def _silu(x):
    h = 0.5 * x
    return h + h * jnp.tanh(h)


def _log_sigmoid(z):
    return jnp.minimum(z, 0.0) - jnp.log1p(jnp.exp(-jnp.abs(z)))


def _gelu_exact(x):
    return 0.5 * x * (1.0 + lax.erf(x * (1.0 / math.sqrt(2.0))))


def _norm_proj_kernel(x_ref, g_ref, w_ref, wg_ref, o_ref, og_ref, *, col_chunk, lead_cols,
                      lead_scale):
    hn = _rmsnorm(x_ref[...], g_ref[...]).astype(BF16)
    n = w_ref.shape[1]
    for c0 in range(0, n, col_chunk):
        y = jnp.dot(hn, w_ref[:, c0:c0 + col_chunk], preferred_element_type=F32)
        if c0 + col_chunk <= lead_cols:
            y = y * lead_scale
        o_ref[:, c0:c0 + col_chunk] = y.astype(o_ref.dtype)
    og_ref[...] = jnp.dot(hn, wg_ref[...], preferred_element_type=F32)


def _norm_proj(x2, g, w_all, layer, n, wg, *, tm=512, col_chunk=512, lead_cols=0, lead_scale=1.0):
    t, d = x2.shape
    assert lead_cols % col_chunk == 0
    return pl.pallas_call(
        functools.partial(_norm_proj_kernel, col_chunk=col_chunk, lead_cols=lead_cols,
                          lead_scale=lead_scale),
        grid=(t // tm,),
        in_specs=[pl.BlockSpec((tm, d), lambda i: (i, 0)),
                  _resident((1, d)), _layer_block((d, n), layer), _resident((d, LANES))],
        out_specs=[pl.BlockSpec((tm, n), lambda i: (i, 0)),
                   pl.BlockSpec((tm, LANES), lambda i: (i, 0))],
        out_shape=[jax.ShapeDtypeStruct((t, n), BF16),
                   jax.ShapeDtypeStruct((t, LANES), F32)],
        compiler_params=_params("parallel"),
        name="norm_proj",
    )(x2, g.reshape(1, d), w_all, wg)


def _out_ffn_kernel(a_ref, wo_ref, x_ref, g_ref, wg_ref, wu_ref, wd_ref, fg_ref, o_ref, *,
                    ff_chunk, final_norm):
    x1 = x_ref[...] + jnp.dot(a_ref[...], wo_ref[...], preferred_element_type=F32)
    hn = _rmsnorm(x1, g_ref[...]).astype(BF16)
    d_ff = wg_ref.shape[1]
    acc = x1
    for c0 in range(0, d_ff, ff_chunk):
        gate = jnp.dot(hn, wg_ref[:, c0:c0 + ff_chunk], preferred_element_type=F32)
        up = jnp.dot(hn, wu_ref[:, c0:c0 + ff_chunk], preferred_element_type=F32)
        h = (_silu(gate) * up).astype(BF16)
        acc = acc + jnp.dot(h, wd_ref[c0:c0 + ff_chunk, :], preferred_element_type=F32)
    if final_norm:
        acc = _rmsnorm(acc, fg_ref[...])
    o_ref[...] = acc


def _out_ffn(a, wo_all, wo_layer, x2, g, w_gu, w_down, layer, final_g, *, final_norm, tm=512,
             ff_chunk=256):
    t, d = x2.shape
    k = a.shape[1]
    d_ff = w_down.shape[1]
    return pl.pallas_call(
        functools.partial(_out_ffn_kernel, ff_chunk=ff_chunk, final_norm=final_norm),
        grid=(t // tm,),
        in_specs=[pl.BlockSpec((tm, k), lambda i: (i, 0)), _layer_block((k, d), wo_layer),
                  pl.BlockSpec((tm, d), lambda i: (i, 0)), _resident((1, d)),
                  _layer_block((d, d_ff), layer, 0), _layer_block((d, d_ff), layer, 1),
                  _layer_block((d_ff, d), layer), _resident((1, d))],
        out_specs=pl.BlockSpec((tm, d), lambda i: (i, 0)),
        out_shape=jax.ShapeDtypeStruct((t, d), F32),
        compiler_params=_params("parallel"),
        name="out_ffn",
    )(a, wo_all, x2, g.reshape(1, d), w_gu, w_gu, w_down, final_g.reshape(1, d))


def _gate_scan_kernel(z_ref, bias_ref, tri_ref, o_ref, *, n_forget, segment):
    rows = tri_ref.shape[0]
    lane = lax.broadcasted_iota(jnp.int32, (rows, LANES), 1)
    forget = lane < n_forget
    tri = tri_ref[...]
    run = None
    for r0 in range(0, z_ref.shape[1], rows):
        z = z_ref[0, r0:r0 + rows] + bias_ref[...]
        lf = jnp.where(forget, _log_sigmoid(z), 0.0)
        cum = jnp.dot(tri, lf, preferred_element_type=F32, precision=lax.Precision.HIGHEST)
        if segment > rows:
            if r0 % segment != 0:
                cum = cum + run
            run = cum[rows - 1:rows, :]
        o_ref[0, r0:r0 + rows] = jnp.where(forget, cum, z)


def _gate_scan(z, bias, *, n_forget, segment):
    b, s, _ = z.shape
    rows = C_SPAN
    r = jnp.arange(rows)
    seg = min(segment, rows)
    tri = ((r[:, None] >= r[None, :]) & (r[:, None] // seg == r[None, :] // seg)).astype(F32)
    return pl.pallas_call(
        functools.partial(_gate_scan_kernel, n_forget=n_forget, segment=segment),
        grid=(b,),
        in_specs=[pl.BlockSpec((1, s, LANES), lambda i: (i, 0, 0)),
                  _resident((1, LANES)), _resident((rows, rows))],
        out_specs=pl.BlockSpec((1, s, LANES), lambda i: (i, 0, 0)),
        out_shape=jax.ShapeDtypeStruct((b, s, LANES), F32),
        compiler_params=_params("parallel"),
        name="gate_scan",
    )(z, bias, tri)


def _fox_kernel(q_ref, k_ref, v_ref, crow_ref, o_ref, vaug_ref, ck_ref, acc_ref, *, tq, tk, td):
    i = pl.program_id(2)
    s_len = k_ref.shape[1]
    hd = A_HEAD_DIM
    ones_lane = (hd, 0)

    @pl.when(i == 0)
    def _():
        lane_s = lax.broadcasted_iota(jnp.int32, (s_len, LANES), 1)
        v = v_ref[0]
        for h in range(2):
            vaug_ref[h] = jnp.where(lane_s == ones_lane[h], jnp.ones_like(v), v)
        ck_ref[...] = crow_ref[0, 0] * LOG2E

    lane = lax.broadcasted_iota(jnp.int32, (tq, LANES), 1)
    q = q_ref[0]
    qh = (jnp.where(lane < hd, q, jnp.zeros_like(q)), jnp.where(lane >= hd, q, jnp.zeros_like(q)))
    acc_ref[...] = jnp.zeros_like(acc_ref)

    def step(ms, r0, k0, ncols, masked):
        kb = k_ref[0, pl.ds(k0, ncols), :]
        out = []
        for h in range(2):
            s = lax.dot_general(qh[h][r0:], kb, (((1,), (1,)), ((), ())),
                                preferred_element_type=F32)
            s = s - ck_ref[h:h + 1, pl.ds(k0, ncols)]
            if masked:
                row = lax.broadcasted_iota(jnp.int32, (tq - r0, ncols), 0)
                col = lax.broadcasted_iota(jnp.int32, (tq - r0, ncols), 1)
                s = jnp.where(row >= col, s, -jnp.inf)
            m_old = ms[h][r0:]
            m_new = jnp.maximum(m_old, jnp.max(s, axis=-1, keepdims=True))
            alpha = jnp.exp2(m_old - m_new)
            p = jnp.exp2(s - m_new).astype(BF16)
            acc_ref[h, r0:] = alpha * acc_ref[h, r0:] + jnp.dot(
                p, vaug_ref[h, pl.ds(k0, ncols), :], preferred_element_type=F32)
            out.append(m_new if r0 == 0 else jnp.concatenate([ms[h][:r0], m_new], axis=0))
        return tuple(out)

    ms = (jnp.full((tq, 1), -jnp.inf, F32),) * 2
    ms = lax.fori_loop(0, i * (tq // tk),
                       lambda j, c: step(c, 0, pl.multiple_of(j * tk, tk), tk, False), ms)
    for r0 in range(0, tq, td):
        ms = step(ms, r0, pl.multiple_of(i * tq + r0, td), td, True)
    a0 = acc_ref[0]
    a1 = acc_ref[1]
    o_ref[0] = jnp.where(lane < hd, a0 / a0[:, ones_lane[0]:ones_lane[0] + 1],
                         a1 / a1[:, ones_lane[1]:ones_lane[1] + 1]).astype(o_ref.dtype)


def _fox_attention(qkv, crow, *, tq=512, tk=512, td=256):
    b, s, d3 = qkv.shape
    d = d3 // 3
    pairs = d // LANES
    return pl.pallas_call(
        functools.partial(_fox_kernel, tq=tq, tk=tk, td=td),
        grid=(b, pairs, s // tq),
        in_specs=[pl.BlockSpec((1, tq, LANES), lambda bi, p, i: (bi, i, p)),
                  pl.BlockSpec((1, s, LANES), lambda bi, p, i: (bi, 0, pairs + p)),
                  pl.BlockSpec((1, s, LANES), lambda bi, p, i: (bi, 0, 2 * pairs + p)),
                  pl.BlockSpec((1, 1, 2, s), lambda bi, p, i: (bi, p, 0, 0))],
        out_specs=pl.BlockSpec((1, tq, LANES), lambda bi, p, i: (bi, i, p)),
        out_shape=jax.ShapeDtypeStruct((b, s, d), BF16),
        scratch_shapes=[pltpu.VMEM((2, s, LANES), BF16), pltpu.VMEM((2, s), F32),
                        pltpu.VMEM((2, tq, LANES), F32)],
        compiler_params=_params("parallel", "parallel", "arbitrary"),
        name="fox_attention",
    )(qkv, qkv, qkv, crow)


def _mlstm_kernel(q_ref, k_ref, v_ref, o_ref, gcol_ref, grow_ref, cw_ref, cb_ref, ng_ref,
                  out_ref, c_ref, m_ref, tailq_ref, tailk_ref, *, chunk):
    L = chunk
    d = B_HEAD_DIM
    width = B_HEADS * d

    @pl.when(pl.program_id(1) == 0)
    def _():
        c_ref[...] = jnp.zeros_like(c_ref)
        m_ref[...] = jnp.zeros_like(m_ref)
        tailq_ref[0:8] = jnp.zeros((8, width), F32)
        tailk_ref[0:8] = jnp.zeros((8, width), F32)

    tailq_ref[8:8 + L] = q_ref[0].astype(F32)
    tailk_ref[8:8 + L] = k_ref[0].astype(F32)

    def conv_silu(ext_ref, col0, sl):
        y = cb_ref[:, col0 + sl.start:col0 + sl.stop]
        for t in range(B_CONV):
            off = 8 - (B_CONV - 1) + t
            y = y + cw_ref[t:t + 1, col0 + sl.start:col0 + sl.stop] * ext_ref[off:off + L, sl]
        return _silu(y)

    gcol = gcol_ref[0]
    grow = grow_ref[0, 0]
    row = lax.broadcasted_iota(jnp.int32, (L, L), 0)
    col = lax.broadcasted_iota(jnp.int32, (L, L), 1)
    tri = row >= col
    ones = jnp.ones((L, d), BF16)
    wide = lambda t: jnp.concatenate([t] * (L // d), axis=1)

    for h in range(B_HEADS):
        sl = slice(h * d, (h + 1) * d)
        qh = conv_silu(tailq_ref, 0, sl).astype(BF16)
        kf = conv_silu(tailk_ref, width, sl) * (d ** -0.5)
        kh = kf.astype(BF16)
        vaug = jnp.concatenate([v_ref[0, :, sl], ones], axis=1)
        bc = jnp.broadcast_to(gcol[:, h:h + 1], (L, d))
        ig = jnp.broadcast_to(gcol[:, B_HEADS + h:B_HEADS + h + 1], (L, d))
        br = grow[h:h + 1, :]
        igr = grow[B_HEADS + h:B_HEADS + h + 1, :]
        m_old = m_ref[h:h + 1, :]

        log_d = jnp.where(tri, wide(bc) - br + igr, -jnp.inf)
        m_inter = bc + m_old
        m_t = jnp.maximum(m_inter, jnp.max(log_d, axis=-1, keepdims=True))
        dm = jnp.exp(log_d - wide(m_t))
        a = jnp.exp(m_inter - m_t)
        sqk = lax.dot_general(qh, kh, (((1,), (1,)), ((), ())), preferred_element_type=F32) * dm
        c_aug = c_ref[h]
        qc = jnp.dot(qh, c_aug.astype(BF16), preferred_element_type=F32)
        sv = jnp.dot(sqk.astype(BF16), vaug, preferred_element_type=F32)
        num = a * qc[:, :d] + sv[:, :d]
        den = a * qc[:, d:] + sv[:, d:]
        hc = num * (1.0 / jnp.maximum(jnp.abs(den), jnp.exp(-m_t)))
        hc = hc * lax.rsqrt(jnp.mean(hc * hc, axis=-1, keepdims=True) + EPS)
        gate = _sigmoid(o_ref[0, :, sl].astype(F32))
        out_ref[0, :, sl] = (hc * ng_ref[:, sl] * gate).astype(out_ref.dtype)

        b_last = bc[L - 1:L, :]
        g = b_last - bc + ig
        m_new = jnp.maximum(b_last + m_old, jnp.max(g, axis=0, keepdims=True))
        w = jnp.exp(g - m_new)
        decay = jnp.exp(b_last + m_old - m_new)
        kw = (kf * w).astype(BF16)
        upd = lax.dot_general(kw, vaug, (((0,), (0,)), ((), ())), preferred_element_type=F32)
        c_ref[h] = jnp.concatenate([decay, decay], axis=1) * c_aug + upd
        m_ref[h:h + 1, :] = m_new

    tailq_ref[0:8] = tailq_ref[L:L + 8]
    tailk_ref[0:8] = tailk_ref[L:L + 8]


def _mlstm(proj, gcol, grow, conv_w, conv_b, norm_g, *, chunk):
    b, s, d4 = proj.shape
    d = d4 // 4
    nc = s // chunk
    blk = lambda c0: pl.BlockSpec((1, chunk, d), lambda bi, c: (bi, c, c0))
    return pl.pallas_call(
        functools.partial(_mlstm_kernel, chunk=chunk),
        grid=(b, nc),
        in_specs=[blk(0), blk(1), blk(2), blk(3),
                  pl.BlockSpec((1, chunk, LANES), lambda bi, c: (bi, c, 0)),
                  pl.BlockSpec((1, 1, 2 * B_HEADS, chunk), lambda bi, c: (bi, c, 0, 0)),
                  _resident((B_CONV, 2 * d)), _resident((1, 2 * d)), _resident((1, d))],
        out_specs=pl.BlockSpec((1, chunk, d), lambda bi, c: (bi, c, 0)),
        out_shape=jax.ShapeDtypeStruct((b, s, d), BF16),
        scratch_shapes=[pltpu.VMEM((B_HEADS, B_HEAD_DIM, 2 * B_HEAD_DIM), F32),
                        pltpu.VMEM((B_HEADS, LANES), F32),
                        pltpu.VMEM((8 + chunk, d), F32), pltpu.VMEM((8 + chunk, d), F32)],
        compiler_params=_params("parallel", "arbitrary"),
        name="mlstm",
    )(proj, proj, proj, proj, gcol, grow, conv_w, conv_b.reshape(1, 2 * d), norm_g.reshape(1, d))


def _gmlp_kernel(x_ref, g_ref, wu_ref, wv_ref, lng_ref, lnb_ref, ws_ref, bs_ref, o_ref, v_ref):
    tm = x_ref.shape[0]
    width = wv_ref.shape[1]
    gw = width // C_GROUPS
    hn = _rmsnorm(x_ref[...], g_ref[...]).astype(BF16)
    for c0 in range(0, width, gw):
        v_ref[:, c0:c0 + gw] = _gelu_exact(
            jnp.dot(hn, wv_ref[:, c0:c0 + gw], preferred_element_type=F32))
    v = v_ref[...]
    mu = jnp.mean(v, axis=-1, keepdims=True)
    vc = v - mu
    var = jnp.mean(vc * vc, axis=-1, keepdims=True)
    vn = (vc * lax.rsqrt(var + EPS) * lng_ref[...] + lnb_ref[...]).astype(BF16)
    row = lax.broadcasted_iota(jnp.int32, (C_SPAN, C_SPAN), 0)
    col = lax.broadcasted_iota(jnp.int32, (C_SPAN, C_SPAN), 1)
    mask = (row // CHUNK) >= (col // CHUNK)
    for gi in range(C_GROUPS):
        cs = slice(gi * gw, (gi + 1) * gw)
        ws = jnp.where(mask, ws_ref[gi], 0.0).astype(BF16)
        bias = bs_ref[:, gi:gi + 1]
        u = _gelu_exact(jnp.dot(hn, wu_ref[:, cs], preferred_element_type=F32))
        for r0 in range(0, tm, C_SPAN):
            s = jnp.dot(ws, vn[r0:r0 + C_SPAN, cs], preferred_element_type=F32) + bias
            o_ref[r0:r0 + C_SPAN, cs] = (u[r0:r0 + C_SPAN] * s).astype(o_ref.dtype)


def _gmlp(x2, g, w_in_all, layer, ln_g, ln_b, w_s, bs_t, *, tm=256):
    t, d = x2.shape
    width = w_in_all.shape[2] // 2
    wu = wv = w_in_all
    return pl.pallas_call(
        _gmlp_kernel,
        grid=(t // tm,),
        in_specs=[pl.BlockSpec((tm, d), lambda i: (i, 0)), _resident((1, d)),
                  _layer_block((d, width), layer, 0), _layer_block((d, width), layer, 1),
                  _resident((1, width)), _resident((1, width)),
                  _resident((C_GROUPS, C_SPAN, C_SPAN)), _resident((C_SPAN, LANES))],
        out_specs=pl.BlockSpec((tm, width), lambda i: (i, 0)),
        out_shape=jax.ShapeDtypeStruct((t, width), BF16),
        scratch_shapes=[pltpu.VMEM((tm, width), F32)],
        compiler_params=_params("parallel"),
        name="gmlp",
    )(x2, g.reshape(1, d), wu, wv, ln_g.reshape(1, width), ln_b.reshape(1, width), w_s, bs_t)


def _pad_lanes(a):
    return jnp.pad(a, [(0, 0)] * (a.ndim - 1) + [(0, LANES - a.shape[-1])])


def _mixer_a(x2, g, w_in_all, w_in_f32, j, b_f, b, s):
    d = x2.shape[1]
    qkv, zf = _norm_proj(x2, g, w_in_all, j, 3 * d, _pad_lanes(w_in_f32[j, :, 3 * d:]).astype(BF16),
                         lead_cols=d, lead_scale=A_HEAD_DIM ** -0.5 * LOG2E)
    c = _gate_scan(zf.reshape(b, s, LANES), _pad_lanes(b_f[None, :]),
                   n_forget=A_HEADS, segment=s)[..., :A_HEADS]
    crow = c.reshape(b, s, A_HEADS // 2, 2).transpose(0, 2, 3, 1)
    return _fox_attention(qkv.reshape(b, s, 3 * d), crow).reshape(b * s, d)


def _mixer_b(x2, g, w_in_all, w_in_f32, j, conv_w, conv_b, b_i, b_f, norm_g, b, s):
    d = x2.shape[1]
    chunk = MLSTM_CHUNK
    w_gate = jnp.concatenate([w_in_f32[j, :, 4 * d + B_HEADS:], w_in_f32[j, :, 4 * d:4 * d + B_HEADS]],
                             axis=1)
    proj, zg = _norm_proj(x2, g, w_in_all, j, 4 * d, _pad_lanes(w_gate).astype(BF16))
    bias = _pad_lanes(jnp.concatenate([b_f, b_i])[None, :])
    gcol = _gate_scan(zg.reshape(b, s, LANES), bias, n_forget=B_HEADS, segment=chunk)
    grow = gcol[..., :2 * B_HEADS].reshape(b, s // chunk, chunk, 2 * B_HEADS).transpose(0, 1, 3, 2)
    hs = _mlstm(proj.reshape(b, s, 4 * d), gcol, grow, conv_w, conv_b, norm_g, chunk=chunk)
    return hs.reshape(b * s, d)


def kernel(x, norm1_g, norm2_g, final_g, a_w_in, a_b_f, a_w_out, b_w_in, b_conv_w, b_conv_b,
           b_b_i, b_b_f, b_norm_g, b_w_out, c_w_in, c_ln_g, c_ln_b, c_w_s, c_b_s, c_w_out,
           ffn_w_gu, ffn_w_down):
    b, s, d = x.shape
    depth = norm1_g.shape[0]
    a_in, b_in, c_in = a_w_in.astype(BF16), b_w_in.astype(BF16), c_w_in.astype(BF16)
    w_outs = (a_w_out.astype(BF16), b_w_out.astype(BF16), c_w_out.astype(BF16))
    w_gu, w_down = ffn_w_gu.astype(BF16), ffn_w_down.astype(BF16)
    x2 = x.reshape(b * s, d)
    for i in range(depth):
        kind, j = i % 3, i // 3
        if kind == 0:
            a = _mixer_a(x2, norm1_g[i], a_in, a_w_in, j, a_b_f[j], b, s)
        elif kind == 1:
            a = _mixer_b(x2, norm1_g[i], b_in, b_w_in, j, b_conv_w[j], b_conv_b[j], b_b_i[j],
                         b_b_f[j], b_norm_g[j], b, s)
        else:
            a = _gmlp(x2, norm1_g[i], c_in, j, c_ln_g[j], c_ln_b[j], c_w_s[j], _pad_lanes(c_b_s[j].T))
        x2 = _out_ffn(a, w_outs[kind], j, x2, norm2_g[i], w_gu, w_down, i, final_g,
                      final_norm=(i == depth - 1))
    return x2.reshape(b, s, d)
```

```python
import functools
import math

import jax
import jax.numpy as jnp
from jax import lax
from jax.experimental import pallas as pl
from jax.experimental.pallas import tpu as pltpu

F32 = jnp.float32
BF16 = jnp.bfloat16

EPS = 1e-6
LOG2E = math.log2(math.e)
LANES = 128
A_HEADS = 16
A_HEAD_DIM = 64
B_HEADS = 8
B_HEAD_DIM = 128
B_CONV = 4
C_GROUPS = 8
C_SPAN = 128
CHUNK = 64
MLSTM_CHUNK = 256
VMEM_LIMIT = 56 * 1024 * 1024


def _params(*sem, flags=None):
    return pltpu.CompilerParams(dimension_semantics=sem, vmem_limit_bytes=VMEM_LIMIT, flags=flags)


def _resident(shape):
    nd = len(shape)
    return pl.BlockSpec(shape, lambda *_: (0,) * nd, pipeline_mode=pl.Buffered(1))


def _layer_block(shape, layer, col_block=0):
    return pl.BlockSpec((None,) + tuple(shape), lambda *_: (layer, 0, col_block),
                        pipeline_mode=pl.Buffered(1))


def _rmsnorm(x, g):
    ms = jnp.mean(x * x, axis=-1, keepdims=True)
    return x * lax.rsqrt(ms + EPS) * g


def _sigmoid(x):
    return 0.5 + 0.5 * jnp.tanh(0.5 * x)


def _silu(x):
    h = 0.5 * x
    return h + h * jnp.tanh(h)


def _log_sigmoid(z):
    return jnp.minimum(z, 0.0) - jnp.log1p(jnp.exp(-jnp.abs(z)))


def _gelu_exact(x):
    return 0.5 * x * (1.0 + lax.erf(x * (1.0 / math.sqrt(2.0))))


def _norm_proj_kernel(x_ref, g_ref, w_ref, wg_ref, o_ref, og_ref, *, col_chunk, lead_cols,
                      lead_scale):
    hn = _rmsnorm(x_ref[...], g_ref[...]).astype(BF16)
    n = w_ref.shape[1]
    for c0 in range(0, n, col_chunk):
        y = jnp.dot(hn, w_ref[:, c0:c0 + col_chunk], preferred_element_type=F32)
        if c0 + col_chunk <= lead_cols:
            y = y * lead_scale
        o_ref[:, c0:c0 + col_chunk] = y.astype(o_ref.dtype)
    og_ref[...] = jnp.dot(hn, wg_ref[...], preferred_element_type=F32)


def _norm_proj(x2, g, w_all, layer, n, wg, *, tm=512, col_chunk=512, lead_cols=0, lead_scale=1.0):
    t, d = x2.shape
    assert lead_cols % col_chunk == 0
    return pl.pallas_call(
        functools.partial(_norm_proj_kernel, col_chunk=col_chunk, lead_cols=lead_cols,
                          lead_scale=lead_scale),
        grid=(t // tm,),
        in_specs=[pl.BlockSpec((tm, d), lambda i: (i, 0)),
                  _resident((1, d)), _layer_block((d, n), layer), _resident((d, LANES))],
        out_specs=[pl.BlockSpec((tm, n), lambda i: (i, 0)),
                   pl.BlockSpec((tm, LANES), lambda i: (i, 0))],
        out_shape=[jax.ShapeDtypeStruct((t, n), BF16),
                   jax.ShapeDtypeStruct((t, LANES), F32)],
        compiler_params=_params("parallel"),
        name="norm_proj",
    )(x2, g.reshape(1, d), w_all, wg)


def _out_ffn_kernel(a_ref, wo_ref, x_ref, g_ref, wg_ref, wu_ref, wd_ref, fg_ref, o_ref, *,
                    ff_chunk, final_norm):
    x1 = x_ref[...] + jnp.dot(a_ref[...], wo_ref[...], preferred_element_type=F32)
    hn = _rmsnorm(x1, g_ref[...]).astype(BF16)
    d_ff = wg_ref.shape[1]
    acc = x1
    for c0 in range(0, d_ff, ff_chunk):
        gate = jnp.dot(hn, wg_ref[:, c0:c0 + ff_chunk], preferred_element_type=F32)
        up = jnp.dot(hn, wu_ref[:, c0:c0 + ff_chunk], preferred_element_type=F32)
        h = (_silu(gate) * up).astype(BF16)
        acc = acc + jnp.dot(h, wd_ref[c0:c0 + ff_chunk, :], preferred_element_type=F32)
    if final_norm:
        acc = _rmsnorm(acc, fg_ref[...])
    o_ref[...] = acc


def _out_ffn(a, wo_all, wo_layer, x2, g, w_gu, w_down, layer, final_g, *, final_norm, tm=512,
             ff_chunk=256):
    t, d = x2.shape
    k = a.shape[1]
    d_ff = w_down.shape[1]
    return pl.pallas_call(
        functools.partial(_out_ffn_kernel, ff_chunk=ff_chunk, final_norm=final_norm),
        grid=(t // tm,),
        in_specs=[pl.BlockSpec((tm, k), lambda i: (i, 0)), _layer_block((k, d), wo_layer),
                  pl.BlockSpec((tm, d), lambda i: (i, 0)), _resident((1, d)),
                  _layer_block((d, d_ff), layer, 0), _layer_block((d, d_ff), layer, 1),
                  _layer_block((d_ff, d), layer), _resident((1, d))],
        out_specs=pl.BlockSpec((tm, d), lambda i: (i, 0)),
        out_shape=jax.ShapeDtypeStruct((t, d), F32),
        compiler_params=_params("parallel"),
        name="out_ffn",
    )(a, wo_all, x2, g.reshape(1, d), w_gu, w_gu, w_down, final_g.reshape(1, d))


def _gate_scan_kernel(z_ref, bias_ref, tri_ref, o_ref, *, n_forget, segment):
    rows = tri_ref.shape[0]
    lane = lax.broadcasted_iota(jnp.int32, (rows, LANES), 1)
    forget = lane < n_forget
    tri = tri_ref[...]
    run = None
    for r0 in range(0, z_ref.shape[1], rows):
        z = z_ref[0, r0:r0 + rows] + bias_ref[...]
        lf = jnp.where(forget, _log_sigmoid(z), 0.0)
        cum = jnp.dot(tri, lf, preferred_element_type=F32, precision=lax.Precision.HIGHEST)
        if segment > rows:
            if r0 % segment != 0:
                cum = cum + run
            run = cum[rows - 1:rows, :]
        o_ref[0, r0:r0 + rows] = jnp.where(forget, cum, z)


def _gate_scan(z, bias, *, n_forget, segment):
    b, s, _ = z.shape
    rows = C_SPAN
    r = jnp.arange(rows)
    seg = min(segment, rows)
    tri = ((r[:, None] >= r[None, :]) & (r[:, None] // seg == r[None, :] // seg)).astype(F32)
    return pl.pallas_call(
        functools.partial(_gate_scan_kernel, n_forget=n_forget, segment=segment),
        grid=(b,),
        in_specs=[pl.BlockSpec((1, s, LANES), lambda i: (i, 0, 0)),
                  _resident((1, LANES)), _resident((rows, rows))],
        out_specs=pl.BlockSpec((1, s, LANES), lambda i: (i, 0, 0)),
        out_shape=jax.ShapeDtypeStruct((b, s, LANES), F32),
        compiler_params=_params("parallel"),
        name="gate_scan",
    )(z, bias, tri)


def _fox_kernel(q_ref, k_ref, v_ref, crow_ref, o_ref, vaug_ref, ck_ref, acc_ref, s_ref, *, tq, tk):
    i = pl.program_id(2)
    s_len = k_ref.shape[1]
    hd = A_HEAD_DIM
    pairs = q_ref.shape[2] // LANES
    heads = 2 * pairs
    ones_lane = (hd, 0)

    @pl.when(i == 0)
    def _():
        lane_s = lax.broadcasted_iota(jnp.int32, (s_len, LANES), 1)
        for hh in range(heads):
            v = v_ref[0, :, (hh // 2) * LANES:(hh // 2 + 1) * LANES]
            vaug_ref[hh] = jnp.where(lane_s == ones_lane[hh % 2], jnp.ones_like(v), v)
        for p in range(pairs):
            ck_ref[2 * p:2 * p + 2] = crow_ref[0, p] * LOG2E

    lane_q = lax.broadcasted_iota(jnp.int32, (tq, pairs * LANES), 1)
    q = q_ref[0]
    qh = [jnp.where((lane_q >= hh * hd) & (lane_q < (hh + 1) * hd), q, jnp.zeros_like(q))
          for hh in range(heads)]
    acc_ref[...] = jnp.zeros_like(acc_ref)

    def scores(k0, r0=0):
        kb = k_ref[0, pl.ds(k0, tk), :]
        return [lax.dot_general(qh[hh][r0:], kb, (((1,), (1,)), ((), ())),
                                preferred_element_type=F32) for hh in range(heads)]

    def update(ms, ss, k0, r0=0, masked=False):
        out = []
        for hh in range(heads):
            s = ss[hh] - ck_ref[hh:hh + 1, pl.ds(k0, tk)]
            if masked:
                row = lax.broadcasted_iota(jnp.int32, (tq - r0, tk), 0)
                col = lax.broadcasted_iota(jnp.int32, (tq - r0, tk), 1)
                s = jnp.where(row >= col, s, -jnp.inf)
            m_old = ms[hh][r0:]
            m_new = jnp.maximum(m_old, jnp.max(s, axis=-1, keepdims=True))
            alpha = jnp.exp2(m_old - m_new)
            p = jnp.exp2(s - m_new).astype(BF16)
            acc_ref[hh, r0:] = alpha * acc_ref[hh, r0:] + jnp.dot(
                p, vaug_ref[hh, pl.ds(k0, tk), :], preferred_element_type=F32)
            out.append(m_new if r0 == 0 else jnp.concatenate([ms[hh][:r0], m_new], axis=0))
        return tuple(out)

    def start(blk):
        return pl.multiple_of(blk * tk, tk)

    def trip(t, ms):
        s_odd = scores(start(2 * t + 1))
        ms = update(ms, [s_ref[hh] for hh in range(heads)], start(2 * t))
        s_next = scores(start(2 * t + 2))
        for hh in range(heads):
            s_ref[hh] = s_next[hh]
        return update(ms, s_odd, start(2 * t + 1))

    assert tq == 2 * tk
    s_first = scores(start(0))
    for hh in range(heads):
        s_ref[hh] = s_first[hh]
    ms = (jnp.full((tq, 1), -jnp.inf, F32),) * heads
    ms = lax.fori_loop(0, i, trip, ms)
    s_low = scores(start(2 * i + 1), r0=tk)
    ms = update(ms, [s_ref[hh] for hh in range(heads)], start(2 * i), masked=True)
    ms = update(ms, s_low, start(2 * i + 1), r0=tk, masked=True)
    lane = lax.broadcasted_iota(jnp.int32, (tq, LANES), 1)
    for p in range(pairs):
        a0 = acc_ref[2 * p]
        a1 = acc_ref[2 * p + 1]
        o_ref[0, :, p * LANES:(p + 1) * LANES] = jnp.where(
            lane < hd, a0 / a0[:, ones_lane[0]:ones_lane[0] + 1],
            a1 / a1[:, ones_lane[1]:ones_lane[1] + 1]).astype(o_ref.dtype)


def _fox_attention(qkv, crow, *, pairs=1, tq=512, tk=256):
    b, s, d3 = qkv.shape
    d = d3 // 3
    w = pairs * LANES
    nblk = d // w
    return pl.pallas_call(
        functools.partial(_fox_kernel, tq=tq, tk=tk),
        grid=(b, nblk, s // tq),
        in_specs=[pl.BlockSpec((1, tq, w), lambda bi, p, i: (bi, i, p)),
                  pl.BlockSpec((1, s, w), lambda bi, p, i: (bi, 0, nblk + p)),
                  pl.BlockSpec((1, s, w), lambda bi, p, i: (bi, 0, 2 * nblk + p)),
                  pl.BlockSpec((1, pairs, 2, s), lambda bi, p, i: (bi, p, 0, 0))],
        out_specs=pl.BlockSpec((1, tq, w), lambda bi, p, i: (bi, i, p)),
        out_shape=jax.ShapeDtypeStruct((b, s, d), BF16),
        scratch_shapes=[pltpu.VMEM((2 * pairs, s, LANES), BF16), pltpu.VMEM((2 * pairs, s), F32),
                        pltpu.VMEM((2 * pairs, tq, LANES), F32),
                        pltpu.VMEM((2 * pairs, tq, tk), F32)],
        compiler_params=_params("parallel", "parallel", "arbitrary"),
        name="fox_attention",
    )(qkv, qkv, qkv, crow)


def _mlstm_kernel(q_ref, k_ref, v_ref, o_ref, gcol_ref, grow_ref, cw_ref, cb_ref, ng_ref,
                  out_ref, c_ref, m_ref, tailq_ref, tailk_ref, *, chunk):
    L = chunk
    d = B_HEAD_DIM
    width = B_HEADS * d

    @pl.when(pl.program_id(1) == 0)
    def _():
        c_ref[...] = jnp.zeros_like(c_ref)
        m_ref[...] = jnp.zeros_like(m_ref)
        tailq_ref[0:8] = jnp.zeros((8, width), F32)
        tailk_ref[0:8] = jnp.zeros((8, width), F32)

    tailq_ref[8:8 + L] = q_ref[0].astype(F32)
    tailk_ref[8:8 + L] = k_ref[0].astype(F32)

    def conv_silu(ext_ref, col0, sl):
        y = cb_ref[:, col0 + sl.start:col0 + sl.stop]
        for t in range(B_CONV):
            off = 8 - (B_CONV - 1) + t
            y = y + cw_ref[t:t + 1, col0 + sl.start:col0 + sl.stop] * ext_ref[off:off + L, sl]
        return _silu(y)

    gcol = gcol_ref[0]
    grow = grow_ref[0, 0]
    row = lax.broadcasted_iota(jnp.int32, (L, L), 0)
    col = lax.broadcasted_iota(jnp.int32, (L, L), 1)
    tri = row >= col
    ones = jnp.ones((L, d), BF16)
    wide = lambda t: jnp.concatenate([t] * (L // d), axis=1)

    for h in range(B_HEADS):
        sl = slice(h * d, (h + 1) * d)
        qh = conv_silu(tailq_ref, 0, sl).astype(BF16)
        kf = conv_silu(tailk_ref, width, sl) * (d ** -0.5)
        kh = kf.astype(BF16)
        vaug = jnp.concatenate([v_ref[0, :, sl], ones], axis=1)
        bc = jnp.broadcast_to(gcol[:, h:h + 1], (L, d))
        ig = jnp.broadcast_to(gcol[:, B_HEADS + h:B_HEADS + h + 1], (L, d))
        br = grow[h:h + 1, :]
        igr = grow[B_HEADS + h:B_HEADS + h + 1, :]
        m_old = m_ref[h:h + 1, :]

        log_d = jnp.where(tri, wide(bc) - br + igr, -jnp.inf)
        m_inter = bc + m_old
        m_t = jnp.maximum(m_inter, jnp.max(log_d, axis=-1, keepdims=True))
        dm = jnp.exp(log_d - wide(m_t))
        a = jnp.exp(m_inter - m_t)
        sqk = lax.dot_general(qh, kh, (((1,), (1,)), ((), ())), preferred_element_type=F32) * dm
        c_aug = c_ref[h]
        qc = jnp.dot(qh, c_aug.astype(BF16), preferred_element_type=F32)
        sv = jnp.dot(sqk.astype(BF16), vaug, preferred_element_type=F32)
        num = a * qc[:, :d] + sv[:, :d]
        den = a * qc[:, d:] + sv[:, d:]
        hc = num * (1.0 / jnp.maximum(jnp.abs(den), jnp.exp(-m_t)))
        hc = hc * lax.rsqrt(jnp.mean(hc * hc, axis=-1, keepdims=True) + EPS)
        gate = _sigmoid(o_ref[0, :, sl].astype(F32))
        out_ref[0, :, sl] = (hc * ng_ref[:, sl] * gate).astype(out_ref.dtype)

        b_last = bc[L - 1:L, :]
        g = b_last - bc + ig
        m_new = jnp.maximum(b_last + m_old, jnp.max(g, axis=0, keepdims=True))
        w = jnp.exp(g - m_new)
        decay = jnp.exp(b_last + m_old - m_new)
        kw = (kf * w).astype(BF16)
        upd = lax.dot_general(kw, vaug, (((0,), (0,)), ((), ())), preferred_element_type=F32)
        c_ref[h] = jnp.concatenate([decay, decay], axis=1) * c_aug + upd
        m_ref[h:h + 1, :] = m_new

    tailq_ref[0:8] = tailq_ref[L:L + 8]
    tailk_ref[0:8] = tailk_ref[L:L + 8]


def _mlstm(proj, gcol, grow, conv_w, conv_b, norm_g, *, chunk):
    b, s, d4 = proj.shape
    d = d4 // 4
    nc = s // chunk
    blk = lambda c0: pl.BlockSpec((1, chunk, d), lambda bi, c: (bi, c, c0))
    return pl.pallas_call(
        functools.partial(_mlstm_kernel, chunk=chunk),
        grid=(b, nc),
        in_specs=[blk(0), blk(1), blk(2), blk(3),
                  pl.BlockSpec((1, chunk, LANES), lambda bi, c: (bi, c, 0)),
                  pl.BlockSpec((1, 1, 2 * B_HEADS, chunk), lambda bi, c: (bi, c, 0, 0)),
                  _resident((B_CONV, 2 * d)), _resident((1, 2 * d)), _resident((1, d))],
        out_specs=pl.BlockSpec((1, chunk, d), lambda bi, c: (bi, c, 0)),
        out_shape=jax.ShapeDtypeStruct((b, s, d), BF16),
        scratch_shapes=[pltpu.VMEM((B_HEADS, B_HEAD_DIM, 2 * B_HEAD_DIM), F32),
                        pltpu.VMEM((B_HEADS, LANES), F32),
                        pltpu.VMEM((8 + chunk, d), F32), pltpu.VMEM((8 + chunk, d), F32)],
        compiler_params=_params("parallel", "arbitrary"),
        name="mlstm",
    )(proj, proj, proj, proj, gcol, grow, conv_w, conv_b.reshape(1, 2 * d), norm_g.reshape(1, d))


def _gmlp_kernel(x_ref, g_ref, wu_ref, wv_ref, lng_ref, lnb_ref, ws_ref, bs_ref, o_ref, v_ref):
    tm = x_ref.shape[0]
    width = wv_ref.shape[1]
    gw = width // C_GROUPS
    hn = _rmsnorm(x_ref[...], g_ref[...]).astype(BF16)
    for c0 in range(0, width, gw):
        v_ref[:, c0:c0 + gw] = _gelu_exact(
            jnp.dot(hn, wv_ref[:, c0:c0 + gw], preferred_element_type=F32))
    v = v_ref[...]
    mu = jnp.mean(v, axis=-1, keepdims=True)
    vc = v - mu
    var = jnp.mean(vc * vc, axis=-1, keepdims=True)
    vn = (vc * lax.rsqrt(var + EPS) * lng_ref[...] + lnb_ref[...]).astype(BF16)
    row = lax.broadcasted_iota(jnp.int32, (C_SPAN, C_SPAN), 0)
    col = lax.broadcasted_iota(jnp.int32, (C_SPAN, C_SPAN), 1)
    mask = (row // CHUNK) >= (col // CHUNK)
    for gi in range(C_GROUPS):
        cs = slice(gi * gw, (gi + 1) * gw)
        ws = jnp.where(mask, ws_ref[gi], 0.0).astype(BF16)
        bias = bs_ref[:, gi:gi + 1]
        u = _gelu_exact(jnp.dot(hn, wu_ref[:, cs], preferred_element_type=F32))
        for r0 in range(0, tm, C_SPAN):
            s = jnp.dot(ws, vn[r0:r0 + C_SPAN, cs], preferred_element_type=F32) + bias
            o_ref[r0:r0 + C_SPAN, cs] = (u[r0:r0 + C_SPAN] * s).astype(o_ref.dtype)


def _gmlp(x2, g, w_in_all, layer, ln_g, ln_b, w_s, bs_t, *, tm=512):
    t, d = x2.shape
    width = w_in_all.shape[2] // 2
    wu = wv = w_in_all
    return pl.pallas_call(
        _gmlp_kernel,
        grid=(t // tm,),
        in_specs=[pl.BlockSpec((tm, d), lambda i: (i, 0)), _resident((1, d)),
                  _layer_block((d, width), layer, 0), _layer_block((d, width), layer, 1),
                  _resident((1, width)), _resident((1, width)),
                  _resident((C_GROUPS, C_SPAN, C_SPAN)), _resident((C_SPAN, LANES))],
        out_specs=pl.BlockSpec((tm, width), lambda i: (i, 0)),
        out_shape=jax.ShapeDtypeStruct((t, width), BF16),
        scratch_shapes=[pltpu.VMEM((tm, width), F32)],
        compiler_params=_params("parallel"),
        name="gmlp",
    )(x2, g.reshape(1, d), wu, wv, ln_g.reshape(1, width), ln_b.reshape(1, width), w_s, bs_t)


def _pad_lanes(a):
    return jnp.pad(a, [(0, 0)] * (a.ndim - 1) + [(0, LANES - a.shape[-1])])


def _mixer_a(x2, g, w_in_all, w_in_f32, j, b_f, b, s):
    d = x2.shape[1]
    qkv, zf = _norm_proj(x2, g, w_in_all, j, 3 * d, _pad_lanes(w_in_f32[j, :, 3 * d:]).astype(BF16),
                         lead_cols=d, lead_scale=A_HEAD_DIM ** -0.5 * LOG2E)
    c = _gate_scan(zf.reshape(b, s, LANES), _pad_lanes(b_f[None, :]),
                   n_forget=A_HEADS, segment=s)[..., :A_HEADS]
    crow = c.reshape(b, s, A_HEADS // 2, 2).transpose(0, 2, 3, 1)
    return _fox_attention(qkv.reshape(b, s, 3 * d), crow).reshape(b * s, d)


def _mixer_b(x2, g, w_in_all, w_in_f32, j, conv_w, conv_b, b_i, b_f, norm_g, b, s):
    d = x2.shape[1]
    chunk = MLSTM_CHUNK
    w_gate = jnp.concatenate([w_in_f32[j, :, 4 * d + B_HEADS:], w_in_f32[j, :, 4 * d:4 * d + B_HEADS]],
                             axis=1)
    proj, zg = _norm_proj(x2, g, w_in_all, j, 4 * d, _pad_lanes(w_gate).astype(BF16))
    bias = _pad_lanes(jnp.concatenate([b_f, b_i])[None, :])
    gcol = _gate_scan(zg.reshape(b, s, LANES), bias, n_forget=B_HEADS, segment=chunk)
    grow = gcol[..., :2 * B_HEADS].reshape(b, s // chunk, chunk, 2 * B_HEADS).transpose(0, 1, 3, 2)
    hs = _mlstm(proj.reshape(b, s, 4 * d), gcol, grow, conv_w, conv_b, norm_g, chunk=chunk)
    return hs.reshape(b * s, d)


def kernel(x, norm1_g, norm2_g, final_g, a_w_in, a_b_f, a_w_out, b_w_in, b_conv_w, b_conv_b,
           b_b_i, b_b_f, b_norm_g, b_w_out, c_w_in, c_ln_g, c_ln_b, c_w_s, c_b_s, c_w_out,
           ffn_w_gu, ffn_w_down):
    b, s, d = x.shape
    depth = norm1_g.shape[0]
    a_in = a_w_in[:, :, :3 * d].astype(BF16)
    b_in = b_w_in[:, :, :4 * d].astype(BF16)
    c_in = c_w_in.astype(BF16)
    w_outs = (a_w_out.astype(BF16), b_w_out.astype(BF16), c_w_out.astype(BF16))
    w_gu, w_down = ffn_w_gu.astype(BF16), ffn_w_down.astype(BF16)
    x2 = x.reshape(b * s, d)
    for i in range(depth):
        kind, j = i % 3, i // 3
        if kind == 0:
            a = _mixer_a(x2, norm1_g[i], a_in, a_w_in, j, a_b_f[j], b, s)
        elif kind == 1:
            a = _mixer_b(x2, norm1_g[i], b_in, b_w_in, j, b_conv_w[j], b_conv_b[j], b_b_i[j],
                         b_b_f[j], b_norm_g[j], b, s)
        else:
            a = _gmlp(x2, norm1_g[i], c_in, j, c_ln_g[j], c_ln_b[j], c_w_s[j], _pad_lanes(c_b_s[j].T))
        x2 = _out_ffn(a, w_outs[kind], j, x2, norm2_g[i], w_gu, w_down, i, final_g,
                      final_norm=(i == depth - 1))
    return x2.reshape(b, s, d)
```

```python
import functools
import math

import jax
import jax.numpy as jnp
from jax import lax
from jax.experimental import pallas as pl
from jax.experimental.pallas import tpu as pltpu

F32 = jnp.float32
BF16 = jnp.bfloat16

EPS = 1e-6
LOG2E = math.log2(math.e)
LANES = 128
A_HEADS = 16
A_HEAD_DIM = 64
B_HEADS = 8
B_HEAD_DIM = 128
B_CONV = 4
C_GROUPS = 8
C_SPAN = 128
CHUNK = 64
MLSTM_CHUNK = 256
VMEM_LIMIT = 56 * 1024 * 1024


def _params(*sem, flags=None):
    return pltpu.CompilerParams(dimension_semantics=sem, vmem_limit_bytes=VMEM_LIMIT, flags=flags)


def _resident(shape):
    nd = len(shape)
    return pl.BlockSpec(shape, lambda *_: (0,) * nd, pipeline_mode=pl.Buffered(1))


def _layer_block(shape, layer, col_block=0):
    return pl.BlockSpec((None,) + tuple(shape), lambda *_: (layer, 0, col_block),
                        pipeline_mode=pl.Buffered(1))


def _rmsnorm(x, g):
    ms = jnp.mean(x * x, axis=-1, keepdims=True)
    return x * lax.rsqrt(ms + EPS) * g


def _sigmoid(x):
    return 0.5 + 0.5 * jnp.tanh(0.5 * x)


def _silu(x):
    h = 0.5 * x
    return h + h * jnp.tanh(h)


def _log_sigmoid(z):
    return jnp.minimum(z, 0.0) - jnp.log1p(jnp.exp(-jnp.abs(z)))


def _gelu_exact(x):
    return 0.5 * x * (1.0 + lax.erf(x * (1.0 / math.sqrt(2.0))))


def _norm_proj_kernel(x_ref, g_ref, w_ref, wg_ref, o_ref, og_ref, *, col_chunk, lead_cols,
                      lead_scale):
    n = w_ref.shape[1]
    tm = x_ref.shape[0]
    half = tm // 2
    for r0 in range(0, tm, half):
        rs = slice(r0, r0 + half)
        hn = _rmsnorm(x_ref[rs, :], g_ref[...]).astype(BF16)
        for c0 in range(0, n, col_chunk):
            y = jnp.dot(hn, w_ref[:, c0:c0 + col_chunk], preferred_element_type=F32)
            if c0 + col_chunk <= lead_cols:
                y = y * lead_scale
            o_ref[rs, c0:c0 + col_chunk] = y.astype(o_ref.dtype)
        og_ref[rs, :] = jnp.dot(hn, wg_ref[...], preferred_element_type=F32)


def _norm_proj(x2, g, w_all, layer, n, wg, *, tm=512, col_chunk=512, lead_cols=0, lead_scale=1.0):
    t, d = x2.shape
    assert lead_cols % col_chunk == 0
    return pl.pallas_call(
        functools.partial(_norm_proj_kernel, col_chunk=col_chunk, lead_cols=lead_cols,
                          lead_scale=lead_scale),
        grid=(t // tm,),
        in_specs=[pl.BlockSpec((tm, d), lambda i: (i, 0)),
                  _resident((1, d)), _layer_block((d, n), layer), _resident((d, LANES))],
        out_specs=[pl.BlockSpec((tm, n), lambda i: (i, 0)),
                   pl.BlockSpec((tm, LANES), lambda i: (i, 0))],
        out_shape=[jax.ShapeDtypeStruct((t, n), BF16),
                   jax.ShapeDtypeStruct((t, LANES), F32)],
        compiler_params=_params("parallel"),
        name="norm_proj",
    )(x2, g.reshape(1, d), w_all, wg)


def _norm_proj_a_kernel(x_ref, g_ref, w_ref, wvt_ref, wg_ref, o_ref, vt_ref, og_ref, *, col_chunk,
                        lead_cols, lead_scale):
    n = w_ref.shape[1]
    d_v = wvt_ref.shape[0]
    tm = x_ref.shape[0]
    half = tm // 2
    for r0 in range(0, tm, half):
        rs = slice(r0, r0 + half)
        hn = _rmsnorm(x_ref[rs, :], g_ref[...]).astype(BF16)
        for c0 in range(0, n, col_chunk):
            y = jnp.dot(hn, w_ref[:, c0:c0 + col_chunk], preferred_element_type=F32)
            if c0 + col_chunk <= lead_cols:
                y = y * lead_scale
            o_ref[rs, c0:c0 + col_chunk] = y.astype(o_ref.dtype)
        for c0 in range(0, d_v, col_chunk):
            yt = lax.dot_general(wvt_ref[c0:c0 + col_chunk, :], hn, (((1,), (1,)), ((), ())),
                                 preferred_element_type=F32)
            vt_ref[c0:c0 + col_chunk, rs] = yt.astype(vt_ref.dtype)
        og_ref[rs, :] = jnp.dot(hn, wg_ref[...], preferred_element_type=F32)


def _norm_proj_a(x2, g, w_all, layer, n, wvt, wg, *, lead_cols, lead_scale, tm=512, col_chunk=512):
    t, d = x2.shape
    d_v = wvt.shape[0]
    return pl.pallas_call(
        functools.partial(_norm_proj_a_kernel, col_chunk=col_chunk, lead_cols=lead_cols,
                          lead_scale=lead_scale),
        grid=(t // tm,),
        in_specs=[pl.BlockSpec((tm, d), lambda i: (i, 0)),
                  _resident((1, d)), _layer_block((d, n), layer), _resident((d_v, d)),
                  _resident((d, LANES))],
        out_specs=[pl.BlockSpec((tm, n), lambda i: (i, 0)),
                   pl.BlockSpec((d_v, tm), lambda i: (0, i)),
                   pl.BlockSpec((tm, LANES), lambda i: (i, 0))],
        out_shape=[jax.ShapeDtypeStruct((t, n), BF16),
                   jax.ShapeDtypeStruct((d_v, t), BF16),
                   jax.ShapeDtypeStruct((t, LANES), F32)],
        compiler_params=_params("parallel"),
        name="norm_proj_a",
    )(x2, g.reshape(1, d), w_all, wvt, wg)


def _out_ffn_kernel(a_ref, wo_ref, x_ref, g_ref, wg_ref, wu_ref, wd_ref, fg_ref, o_ref, *,
                    ff_chunk, final_norm):
    x1 = x_ref[...] + jnp.dot(a_ref[...], wo_ref[...], preferred_element_type=F32)
    hn = _rmsnorm(x1, g_ref[...]).astype(BF16)
    d_ff = wg_ref.shape[1]
    acc = x1
    for c0 in range(0, d_ff, ff_chunk):
        gate = jnp.dot(hn, wg_ref[:, c0:c0 + ff_chunk], preferred_element_type=F32)
        up = jnp.dot(hn, wu_ref[:, c0:c0 + ff_chunk], preferred_element_type=F32)
        h = (_silu(gate) * up).astype(BF16)
        acc = acc + jnp.dot(h, wd_ref[c0:c0 + ff_chunk, :], preferred_element_type=F32)
    if final_norm:
        acc = _rmsnorm(acc, fg_ref[...])
    o_ref[...] = acc


def _out_ffn(a, wo_all, wo_layer, x2, g, w_gu, w_down, layer, final_g, *, final_norm, tm=512,
             ff_chunk=256):
    t, d = x2.shape
    k = a.shape[1]
    d_ff = w_down.shape[1]
    return pl.pallas_call(
        functools.partial(_out_ffn_kernel, ff_chunk=ff_chunk, final_norm=final_norm),
        grid=(t // tm,),
        in_specs=[pl.BlockSpec((tm, k), lambda i: (i, 0)), _layer_block((k, d), wo_layer),
                  pl.BlockSpec((tm, d), lambda i: (i, 0)), _resident((1, d)),
                  _layer_block((d, d_ff), layer, 0), _layer_block((d, d_ff), layer, 1),
                  _layer_block((d_ff, d), layer), _resident((1, d))],
        out_specs=pl.BlockSpec((tm, d), lambda i: (i, 0)),
        out_shape=jax.ShapeDtypeStruct((t, d), F32),
        compiler_params=_params("parallel"),
        name="out_ffn",
    )(a, wo_all, x2, g.reshape(1, d), w_gu, w_gu, w_down, final_g.reshape(1, d))


def _gate_scan_kernel(z_ref, bias_ref, tri_ref, *o_refs, n_forget, segment, split_scale):
    rows = tri_ref.shape[0]
    lane = lax.broadcasted_iota(jnp.int32, (rows, LANES), 1)
    forget = lane < n_forget
    tri = tri_ref[...]
    run = None
    for r0 in range(0, z_ref.shape[1], rows):
        z = z_ref[0, r0:r0 + rows] + bias_ref[...]
        lf = jnp.where(forget, _log_sigmoid(z), 0.0)
        cum = jnp.dot(tri, lf, preferred_element_type=F32, precision=lax.Precision.HIGHEST)
        if segment > rows:
            if r0 % segment != 0:
                cum = cum + run
            run = cum[rows - 1:rows, :]
        rs = slice(r0, r0 + rows)
        if split_scale is None:
            o_refs[0][0, rs] = jnp.where(forget, cum, z)
        else:
            val = cum * split_scale
            for o_ref in o_refs:
                part = val.astype(BF16)
                o_ref[0, rs] = part
                val = val - part.astype(F32)


def _gate_scan(z, bias, *, n_forget, segment, split_scale=None):
    b, s, _ = z.shape
    rows = C_SPAN
    r = jnp.arange(rows)
    seg = min(segment, rows)
    tri = ((r[:, None] >= r[None, :]) & (r[:, None] // seg == r[None, :] // seg)).astype(F32)
    blk = pl.BlockSpec((1, s, LANES), lambda i: (i, 0, 0))
    n_out, dtype = (1, F32) if split_scale is None else (3, BF16)
    out = pl.pallas_call(
        functools.partial(_gate_scan_kernel, n_forget=n_forget, segment=segment,
                          split_scale=split_scale),
        grid=(b,),
        in_specs=[blk, _resident((1, LANES)), _resident((rows, rows))],
        out_specs=[blk] * n_out,
        out_shape=[jax.ShapeDtypeStruct((b, s, LANES), dtype)] * n_out,
        compiler_params=_params("parallel"),
        name="gate_scan",
    )(z, bias, tri)
    return out[0] if split_scale is None else out


def _fox_t_kernel(q_ref, k_ref, cp_ref, vt_ref, o_ref, acc_ref, s_ref, *, tq, tk):
    i = pl.program_id(2)
    hd = A_HEAD_DIM
    ones_row = (hd, 0)

    lane = lax.broadcasted_iota(jnp.int32, (tq, LANES), 1)
    row_v = lax.broadcasted_iota(jnp.int32, (LANES, tk), 0)
    q = q_ref[0].astype(F32)
    q_aug = []
    for h in range(2):
        in_head = (lane >= h * hd) & (lane < (h + 1) * hd)
        bias_sel = (lane >= 3 * h) & (lane < 3 * h + 3)
        q_aug.append(jnp.concatenate(
            [jnp.where(in_head, q, 0.0), jnp.where(bias_sel, -1.0, 0.0)], axis=1).astype(BF16))
    acc_ref[...] = jnp.zeros_like(acc_ref)

    def scores(k0, c0=0):
        kb = jnp.concatenate([k_ref[0, pl.ds(k0, tk), :], cp_ref[0, 0, pl.ds(k0, tk), :]], axis=1)
        return [lax.dot_general(kb, q_aug[h][c0:], (((1,), (1,)), ((), ())),
                                preferred_element_type=F32) for h in range(2)]

    def update(ms, ss, k0, c0=0, masked=False):
        out = []
        for h in range(2):
            s = ss[h]
            if masked:
                row = lax.broadcasted_iota(jnp.int32, (tk, tq - c0), 0)
                col = lax.broadcasted_iota(jnp.int32, (tk, tq - c0), 1)
                s = jnp.where(row <= col, s, -jnp.inf)
            m_old = ms[h][:, c0:]
            m_new = jnp.maximum(m_old, jnp.max(s, axis=0, keepdims=True))
            alpha = jnp.exp2(m_old - m_new)
            p = jnp.exp2(s - m_new).astype(BF16)
            vt = vt_ref[:, pl.ds(k0, tk)]
            vt = jnp.where(row_v == ones_row[h], jnp.ones_like(vt), vt)
            acc_ref[h, :, c0:] = alpha * acc_ref[h, :, c0:] + jnp.dot(
                vt, p, preferred_element_type=F32)
            out.append(m_new if c0 == 0 else jnp.concatenate([ms[h][:, :c0], m_new], axis=1))
        return tuple(out)

    def start(blk):
        return pl.multiple_of(blk * tk, tk)

    def trip(t, ms):
        s_odd = scores(start(2 * t + 1))
        ms = update(ms, [s_ref[h] for h in range(2)], start(2 * t))
        s_next = scores(start(2 * t + 2))
        for h in range(2):
            s_ref[h] = s_next[h]
        return update(ms, s_odd, start(2 * t + 1))

    assert tq == 2 * tk
    s_first = scores(start(0))
    for h in range(2):
        s_ref[h] = s_first[h]
    ms = (jnp.full((1, tq), -jnp.inf, F32),) * 2
    ms = lax.fori_loop(0, i, trip, ms)
    s_low = scores(start(2 * i + 1), c0=tk)
    ms = update(ms, [s_ref[h] for h in range(2)], start(2 * i), masked=True)
    ms = update(ms, s_low, start(2 * i + 1), c0=tk, masked=True)
    outs = []
    for h in range(2):
        a = acc_ref[h]
        outs.append((a * (1.0 / a[ones_row[h]:ones_row[h] + 1, :])).T)
    o_ref[0] = jnp.where(lane < hd, outs[0], outs[1]).astype(o_ref.dtype)


def _fox_attention_t(qk, cparts, vt, *, tq=512, tk=256):
    b, s, d2 = qk.shape
    d = d2 // 2
    nblk = d // LANES
    return pl.pallas_call(
        functools.partial(_fox_t_kernel, tq=tq, tk=tk),
        grid=(b, nblk, s // tq),
        in_specs=[pl.BlockSpec((1, tq, LANES), lambda bi, p, i: (bi, i, p)),
                  pl.BlockSpec((1, s, LANES), lambda bi, p, i: (bi, 0, nblk + p)),
                  pl.BlockSpec((1, 1, s, LANES), lambda bi, p, i: (bi, p, 0, 0)),
                  pl.BlockSpec((LANES, s), lambda bi, p, i: (p, bi))],
        out_specs=pl.BlockSpec((1, tq, LANES), lambda bi, p, i: (bi, i, p)),
        out_shape=jax.ShapeDtypeStruct((b, s, d), BF16),
        scratch_shapes=[pltpu.VMEM((2, LANES, tq), F32), pltpu.VMEM((2, tk, tq), F32)],
        compiler_params=_params("parallel", "parallel", "arbitrary"),
        name="fox_attention",
    )(qk, qk, cparts, vt)


def _mlstm_kernel(q_ref, k_ref, v_ref, o_ref, gcol_ref, grow_ref, cw_ref, cb_ref, ng_ref,
                  out_ref, c_ref, m_ref, tailq_ref, tailk_ref, *, chunk):
    L = chunk
    d = B_HEAD_DIM
    width = B_HEADS * d

    @pl.when(pl.program_id(1) == 0)
    def _():
        c_ref[...] = jnp.zeros_like(c_ref)
        m_ref[...] = jnp.zeros_like(m_ref)
        tailq_ref[0:8] = jnp.zeros((8, width), F32)
        tailk_ref[0:8] = jnp.zeros((8, width), F32)

    tailq_ref[8:8 + L] = q_ref[0].astype(F32)
    tailk_ref[8:8 + L] = k_ref[0].astype(F32)

    def conv_silu(ext_ref, col0, sl):
        y = cb_ref[:, col0 + sl.start:col0 + sl.stop]
        for t in range(B_CONV):
            off = 8 - (B_CONV - 1) + t
            y = y + cw_ref[t:t + 1, col0 + sl.start:col0 + sl.stop] * ext_ref[off:off + L, sl]
        return _silu(y)

    gcol = gcol_ref[0]
    grow = grow_ref[0, 0]
    row = lax.broadcasted_iota(jnp.int32, (L, L), 0)
    col = lax.broadcasted_iota(jnp.int32, (L, L), 1)
    tri = row >= col
    ones = jnp.ones((L, d), BF16)
    wide = lambda t: jnp.concatenate([t] * (L // d), axis=1)

    for h in range(B_HEADS):
        sl = slice(h * d, (h + 1) * d)
        qh = conv_silu(tailq_ref, 0, sl).astype(BF16)
        kf = conv_silu(tailk_ref, width, sl) * (d ** -0.5)
        kh = kf.astype(BF16)
        vaug = jnp.concatenate([v_ref[0, :, sl], ones], axis=1)
        bc = jnp.broadcast_to(gcol[:, h:h + 1], (L, d))
        ig = jnp.broadcast_to(gcol[:, B_HEADS + h:B_HEADS + h + 1], (L, d))
        br = grow[h:h + 1, :]
        igr = grow[B_HEADS + h:B_HEADS + h + 1, :]
        m_old = m_ref[h:h + 1, :]

        log_d = jnp.where(tri, wide(bc) - br + igr, -jnp.inf)
        m_inter = bc + m_old
        m_t = jnp.maximum(m_inter, jnp.max(log_d, axis=-1, keepdims=True))
        dm = jnp.exp(log_d - wide(m_t))
        a = jnp.exp(m_inter - m_t)
        sqk = lax.dot_general(qh, kh, (((1,), (1,)), ((), ())), preferred_element_type=F32) * dm
        c_aug = c_ref[h]
        qc = jnp.dot(qh, c_aug.astype(BF16), preferred_element_type=F32)
        sv = jnp.dot(sqk.astype(BF16), vaug, preferred_element_type=F32)
        num = a * qc[:, :d] + sv[:, :d]
        den = a * qc[:, d:] + sv[:, d:]
        hc = num * (1.0 / jnp.maximum(jnp.abs(den), jnp.exp(-m_t)))
        hc = hc * lax.rsqrt(jnp.mean(hc * hc, axis=-1, keepdims=True) + EPS)
        gate = _sigmoid(o_ref[0, :, sl].astype(F32))
        out_ref[0, :, sl] = (hc * ng_ref[:, sl] * gate).astype(out_ref.dtype)

        b_last = bc[L - 1:L, :]
        g = b_last - bc + ig
        m_new = jnp.maximum(b_last + m_old, jnp.max(g, axis=0, keepdims=True))
        w = jnp.exp(g - m_new)
        decay = jnp.exp(b_last + m_old - m_new)
        kw = (kf * w).astype(BF16)
        upd = lax.dot_general(kw, vaug, (((0,), (0,)), ((), ())), preferred_element_type=F32)
        c_ref[h] = jnp.concatenate([decay, decay], axis=1) * c_aug + upd
        m_ref[h:h + 1, :] = m_new

    tailq_ref[0:8] = tailq_ref[L:L + 8]
    tailk_ref[0:8] = tailk_ref[L:L + 8]


def _mlstm(proj, gcol, grow, conv_w, conv_b, norm_g, *, chunk):
    b, s, d4 = proj.shape
    d = d4 // 4
    nc = s // chunk
    blk = lambda c0: pl.BlockSpec((1, chunk, d), lambda bi, c: (bi, c, c0))
    return pl.pallas_call(
        functools.partial(_mlstm_kernel, chunk=chunk),
        grid=(b, nc),
        in_specs=[blk(0), blk(1), blk(2), blk(3),
                  pl.BlockSpec((1, chunk, LANES), lambda bi, c: (bi, c, 0)),
                  pl.BlockSpec((1, 1, 2 * B_HEADS, chunk), lambda bi, c: (bi, c, 0, 0)),
                  _resident((B_CONV, 2 * d)), _resident((1, 2 * d)), _resident((1, d))],
        out_specs=pl.BlockSpec((1, chunk, d), lambda bi, c: (bi, c, 0)),
        out_shape=jax.ShapeDtypeStruct((b, s, d), BF16),
        scratch_shapes=[pltpu.VMEM((B_HEADS, B_HEAD_DIM, 2 * B_HEAD_DIM), F32),
                        pltpu.VMEM((B_HEADS, LANES), F32),
                        pltpu.VMEM((8 + chunk, d), F32), pltpu.VMEM((8 + chunk, d), F32)],
        compiler_params=_params("parallel", "arbitrary"),
        name="mlstm",
    )(proj, proj, proj, proj, gcol, grow, conv_w, conv_b.reshape(1, 2 * d), norm_g.reshape(1, d))


def _gmlp_kernel(x_ref, g_ref, wu_ref, wv_ref, lng_ref, lnb_ref, ws_ref, bs_ref, o_ref, v_ref):
    tm = x_ref.shape[0]
    width = wv_ref.shape[1]
    gw = width // C_GROUPS
    hn = _rmsnorm(x_ref[...], g_ref[...]).astype(BF16)
    for c0 in range(0, width, gw):
        v_ref[:, c0:c0 + gw] = _gelu_exact(
            jnp.dot(hn, wv_ref[:, c0:c0 + gw], preferred_element_type=F32))
    v = v_ref[...]
    mu = jnp.mean(v, axis=-1, keepdims=True)
    vc = v - mu
    var = jnp.mean(vc * vc, axis=-1, keepdims=True)
    vn = (vc * lax.rsqrt(var + EPS) * lng_ref[...] + lnb_ref[...]).astype(BF16)
    row = lax.broadcasted_iota(jnp.int32, (C_SPAN, C_SPAN), 0)
    col = lax.broadcasted_iota(jnp.int32, (C_SPAN, C_SPAN), 1)
    mask = (row // CHUNK) >= (col // CHUNK)
    for gi in range(C_GROUPS):
        cs = slice(gi * gw, (gi + 1) * gw)
        ws = jnp.where(mask, ws_ref[gi], 0.0).astype(BF16)
        bias = bs_ref[:, gi:gi + 1]
        u = _gelu_exact(jnp.dot(hn, wu_ref[:, cs], preferred_element_type=F32))
        for r0 in range(0, tm, C_SPAN):
            s = jnp.dot(ws, vn[r0:r0 + C_SPAN, cs], preferred_element_type=F32) + bias
            o_ref[r0:r0 + C_SPAN, cs] = (u[r0:r0 + C_SPAN] * s).astype(o_ref.dtype)


def _gmlp(x2, g, w_in_all, layer, ln_g, ln_b, w_s, bs_t, *, tm=512):
    t, d = x2.shape
    width = w_in_all.shape[2] // 2
    wu = wv = w_in_all
    return pl.pallas_call(
        _gmlp_kernel,
        grid=(t // tm,),
        in_specs=[pl.BlockSpec((tm, d), lambda i: (i, 0)), _resident((1, d)),
                  _layer_block((d, width), layer, 0), _layer_block((d, width), layer, 1),
                  _resident((1, width)), _resident((1, width)),
                  _resident((C_GROUPS, C_SPAN, C_SPAN)), _resident((C_SPAN, LANES))],
        out_specs=pl.BlockSpec((tm, width), lambda i: (i, 0)),
        out_shape=jax.ShapeDtypeStruct((t, width), BF16),
        scratch_shapes=[pltpu.VMEM((tm, width), F32)],
        compiler_params=_params("parallel"),
        name="gmlp",
    )(x2, g.reshape(1, d), wu, wv, ln_g.reshape(1, width), ln_b.reshape(1, width), w_s, bs_t)


def _pad_lanes(a):
    return jnp.pad(a, [(0, 0)] * (a.ndim - 1) + [(0, LANES - a.shape[-1])])


def _mixer_a(x2, g, w_in_all, w_in_f32, j, b_f, b, s):
    d = x2.shape[1]
    wvt = w_in_f32[j, :, 2 * d:3 * d].T.astype(BF16)
    qk, vt, zf = _norm_proj_a(x2, g, w_in_all, j, 2 * d, wvt,
                              _pad_lanes(w_in_f32[j, :, 3 * d:]).astype(BF16),
                              lead_cols=d, lead_scale=A_HEAD_DIM ** -0.5 * LOG2E)
    parts = _gate_scan(zf.reshape(b, s, LANES), _pad_lanes(b_f[None, :]),
                       n_forget=A_HEADS, segment=s, split_scale=LOG2E)
    parts = jnp.stack([part[..., :A_HEADS] for part in parts], axis=-1)
    parts = parts.reshape(b, s, A_HEADS // 2, 6).transpose(0, 2, 1, 3)
    return _fox_attention_t(qk.reshape(b, s, 2 * d), _pad_lanes(parts), vt).reshape(b * s, d)


def _mixer_b(x2, g, w_in_all, w_in_f32, j, conv_w, conv_b, b_i, b_f, norm_g, b, s):
    d = x2.shape[1]
    chunk = MLSTM_CHUNK
    w_gate = jnp.concatenate([w_in_f32[j, :, 4 * d + B_HEADS:], w_in_f32[j, :, 4 * d:4 * d + B_HEADS]],
                             axis=1)
    proj, zg = _norm_proj(x2, g, w_in_all, j, 4 * d, _pad_lanes(w_gate).astype(BF16))
    bias = _pad_lanes(jnp.concatenate([b_f, b_i])[None, :])
    gcol = _gate_scan(zg.reshape(b, s, LANES), bias, n_forget=B_HEADS, segment=chunk)
    grow = gcol[..., :2 * B_HEADS].reshape(b, s // chunk, chunk, 2 * B_HEADS).transpose(0, 1, 3, 2)
    hs = _mlstm(proj.reshape(b, s, 4 * d), gcol, grow, conv_w, conv_b, norm_g, chunk=chunk)
    return hs.reshape(b * s, d)


def kernel(x, norm1_g, norm2_g, final_g, a_w_in, a_b_f, a_w_out, b_w_in, b_conv_w, b_conv_b,
           b_b_i, b_b_f, b_norm_g, b_w_out, c_w_in, c_ln_g, c_ln_b, c_w_s, c_b_s, c_w_out,
           ffn_w_gu, ffn_w_down):
    b, s, d = x.shape
    depth = norm1_g.shape[0]
    a_in, b_in, c_in = a_w_in.astype(BF16), b_w_in.astype(BF16), c_w_in.astype(BF16)
    w_outs = (a_w_out.astype(BF16), b_w_out.astype(BF16), c_w_out.astype(BF16))
    w_gu, w_down = ffn_w_gu.astype(BF16), ffn_w_down.astype(BF16)
    x2 = x.reshape(b * s, d)
    for i in range(depth):
        kind, j = i % 3, i // 3
        if kind == 0:
            a = _mixer_a(x2, norm1_g[i], a_in, a_w_in, j, a_b_f[j], b, s)
        elif kind == 1:
            a = _mixer_b(x2, norm1_g[i], b_in, b_w_in, j, b_conv_w[j], b_conv_b[j], b_b_i[j],
                         b_b_f[j], b_norm_g[j], b, s)
        else:
            a = _gmlp(x2, norm1_g[i], c_in, j, c_ln_g[j], c_ln_b[j], c_w_s[j], _pad_lanes(c_b_s[j].T))
        x2 = _out_ffn(a, w_outs[kind], j, x2, norm2_g[i], w_gu, w_down, i, final_g,
                      final_norm=(i == depth - 1))
    return x2.reshape(b, s, d)
```

```python
import functools
import math

import jax
import jax.numpy as jnp
from jax import lax
from jax.experimental import pallas as pl
from jax.experimental.pallas import tpu as pltpu

F32 = jnp.float32
BF16 = jnp.bfloat16

EPS = 1e-6
LOG2E = math.log2(math.e)
LANES = 128
A_HEADS = 16
A_HEAD_DIM = 64
B_HEADS = 8
B_HEAD_DIM = 128
B_CONV = 4
C_GROUPS = 8
C_SPAN = 128
CHUNK = 64
MLSTM_CHUNK = 256
VMEM_LIMIT = 56 * 1024 * 1024


def _params(*sem, flags=None):
    return pltpu.CompilerParams(dimension_semantics=sem, vmem_limit_bytes=VMEM_LIMIT, flags=flags)


def _resident(shape):
    nd = len(shape)
    return pl.BlockSpec(shape, lambda *_: (0,) * nd, pipeline_mode=pl.Buffered(1))


def _layer_block(shape, layer, col_block=0):
    return pl.BlockSpec((None,) + tuple(shape), lambda *_: (layer, 0, col_block),
                        pipeline_mode=pl.Buffered(1))


def _rmsnorm(x, g):
    ms = jnp.mean(x * x, axis=-1, keepdims=True)
    return x * lax.rsqrt(ms + EPS) * g


def _sigmoid(x):
    return 0.5 + 0.5 * jnp.tanh(0.5 * x)


def _silu(x):
    h = 0.5 * x
    return h + h * jnp.tanh(h)


def _log_sigmoid(z):
    return jnp.minimum(z, 0.0) - jnp.log1p(jnp.exp(-jnp.abs(z)))


def _gelu_exact(x):
    return 0.5 * x * (1.0 + lax.erf(x * (1.0 / math.sqrt(2.0))))


def _norm_proj_kernel(x_ref, g_ref, w_ref, wg_ref, o_ref, og_ref, *, col_chunk):
    n = w_ref.shape[1]
    tm = x_ref.shape[0]
    half = tm // 2
    for r0 in range(0, tm, half):
        rs = slice(r0, r0 + half)
        hn = _rmsnorm(x_ref[rs, :], g_ref[...]).astype(BF16)
        for c0 in range(0, n, col_chunk):
            y = jnp.dot(hn, w_ref[:, c0:c0 + col_chunk], preferred_element_type=F32)
            o_ref[rs, c0:c0 + col_chunk] = y.astype(o_ref.dtype)
        og_ref[rs, :] = jnp.dot(hn, wg_ref[...], preferred_element_type=F32)


def _norm_proj(x2, g, w_all, layer, n, wg, *, tm=512, col_chunk=512):
    t, d = x2.shape
    return pl.pallas_call(
        functools.partial(_norm_proj_kernel, col_chunk=col_chunk),
        grid=(t // tm,),
        in_specs=[pl.BlockSpec((tm, d), lambda i: (i, 0)),
                  _resident((1, d)), _layer_block((d, n), layer), _resident((d, LANES))],
        out_specs=[pl.BlockSpec((tm, n), lambda i: (i, 0)),
                   pl.BlockSpec((tm, LANES), lambda i: (i, 0))],
        out_shape=[jax.ShapeDtypeStruct((t, n), BF16),
                   jax.ShapeDtypeStruct((t, LANES), F32)],
        compiler_params=_params("parallel"),
        name="norm_proj",
    )(x2, g.reshape(1, d), w_all, wg)


def _norm_proj_a_kernel(x_ref, g_ref, w_ref, wvt_ref, wg_ref, o_ref, vt_ref, og_ref, *, col_chunk,
                        lead_cols, lead_scale):
    n = w_ref.shape[1]
    d_v = wvt_ref.shape[0]
    tm = x_ref.shape[0]
    half = tm // 2
    for r0 in range(0, tm, half):
        rs = slice(r0, r0 + half)
        hn = _rmsnorm(x_ref[rs, :], g_ref[...]).astype(BF16)
        for c0 in range(0, n, col_chunk):
            y = jnp.dot(hn, w_ref[:, c0:c0 + col_chunk], preferred_element_type=F32)
            if c0 + col_chunk <= lead_cols:
                y = y * lead_scale
            o_ref[rs, c0:c0 + col_chunk] = y.astype(o_ref.dtype)
        for c0 in range(0, d_v, col_chunk):
            yt = lax.dot_general(wvt_ref[c0:c0 + col_chunk, :], hn, (((1,), (1,)), ((), ())),
                                 preferred_element_type=F32)
            vt_ref[c0:c0 + col_chunk, rs] = yt.astype(vt_ref.dtype)
        og_ref[rs, :] = jnp.dot(hn, wg_ref[...], preferred_element_type=F32)


def _norm_proj_a(x2, g, w_all, layer, n, wvt, wg, *, lead_cols, lead_scale, tm=512, col_chunk=512):
    t, d = x2.shape
    d_v = wvt.shape[0]
    return pl.pallas_call(
        functools.partial(_norm_proj_a_kernel, col_chunk=col_chunk, lead_cols=lead_cols,
                          lead_scale=lead_scale),
        grid=(t // tm,),
        in_specs=[pl.BlockSpec((tm, d), lambda i: (i, 0)),
                  _resident((1, d)), _layer_block((d, n), layer), _resident((d_v, d)),
                  _resident((d, LANES))],
        out_specs=[pl.BlockSpec((tm, n), lambda i: (i, 0)),
                   pl.BlockSpec((d_v, tm), lambda i: (0, i)),
                   pl.BlockSpec((tm, LANES), lambda i: (i, 0))],
        out_shape=[jax.ShapeDtypeStruct((t, n), BF16),
                   jax.ShapeDtypeStruct((d_v, t), BF16),
                   jax.ShapeDtypeStruct((t, LANES), F32)],
        compiler_params=_params("parallel"),
        name="norm_proj_a",
    )(x2, g.reshape(1, d), w_all, wvt, wg)


def _out_ffn_kernel(a_ref, wo_ref, x_ref, g_ref, wg_ref, wu_ref, wd_ref, fg_ref, o_ref, *,
                    ff_chunk, final_norm):
    x1 = x_ref[...] + jnp.dot(a_ref[...], wo_ref[...], preferred_element_type=F32)
    hn = _rmsnorm(x1, g_ref[...]).astype(BF16)
    d_ff = wg_ref.shape[1]
    acc = x1
    for c0 in range(0, d_ff, ff_chunk):
        gate = jnp.dot(hn, wg_ref[:, c0:c0 + ff_chunk], preferred_element_type=F32)
        up = jnp.dot(hn, wu_ref[:, c0:c0 + ff_chunk], preferred_element_type=F32)
        h = (_silu(gate) * up).astype(BF16)
        acc = acc + jnp.dot(h, wd_ref[c0:c0 + ff_chunk, :], preferred_element_type=F32)
    if final_norm:
        acc = _rmsnorm(acc, fg_ref[...])
    o_ref[...] = acc


def _out_ffn(a, wo_all, wo_layer, x2, g, w_gu, w_down, layer, final_g, *, final_norm, tm=512,
             ff_chunk=256):
    t, d = x2.shape
    k = a.shape[1]
    d_ff = w_down.shape[1]
    return pl.pallas_call(
        functools.partial(_out_ffn_kernel, ff_chunk=ff_chunk, final_norm=final_norm),
        grid=(t // tm,),
        in_specs=[pl.BlockSpec((tm, k), lambda i: (i, 0)), _layer_block((k, d), wo_layer),
                  pl.BlockSpec((tm, d), lambda i: (i, 0)), _resident((1, d)),
                  _layer_block((d, d_ff), layer, 0), _layer_block((d, d_ff), layer, 1),
                  _layer_block((d_ff, d), layer), _resident((1, d))],
        out_specs=pl.BlockSpec((tm, d), lambda i: (i, 0)),
        out_shape=jax.ShapeDtypeStruct((t, d), F32),
        compiler_params=_params("parallel"),
        name="out_ffn",
    )(a, wo_all, x2, g.reshape(1, d), w_gu, w_gu, w_down, final_g.reshape(1, d))


def _gate_scan_kernel(z_ref, bias_ref, tri_ref, o_ref, *, n_forget, segment, split_scale):
    rows = tri_ref.shape[0]
    lane = lax.broadcasted_iota(jnp.int32, (rows, LANES), 1)
    forget = lane < n_forget
    tri = tri_ref[...]
    run = None
    for r0 in range(0, z_ref.shape[1], rows):
        z = z_ref[0, r0:r0 + rows] + bias_ref[...]
        lf = jnp.where(forget, _log_sigmoid(z), 0.0)
        cum = jnp.dot(tri, lf, preferred_element_type=F32, precision=lax.Precision.HIGHEST)
        if segment > rows:
            if r0 % segment != 0:
                cum = cum + run
            run = cum[rows - 1:rows, :]
        rs = slice(r0, r0 + rows)
        if split_scale is None:
            o_ref[0, rs] = jnp.where(forget, cum, z)
        else:
            group = n_forget // 3
            val = cum * split_scale
            hi = val.astype(BF16).astype(F32)
            mid = (val - hi).astype(BF16).astype(F32)
            lo = val - hi - mid
            o_ref[0, rs] = jnp.where(lane < group, hi,
                                     jnp.where(lane < 2 * group, mid, lo)).astype(o_ref.dtype)


def _gate_scan(z, bias, *, n_forget, segment, split_scale=None):
    b, s, _ = z.shape
    rows = C_SPAN
    r = jnp.arange(rows)
    seg = min(segment, rows)
    tri = ((r[:, None] >= r[None, :]) & (r[:, None] // seg == r[None, :] // seg)).astype(F32)
    blk = pl.BlockSpec((1, s, LANES), lambda i: (i, 0, 0))
    return pl.pallas_call(
        functools.partial(_gate_scan_kernel, n_forget=n_forget, segment=segment,
                          split_scale=split_scale),
        grid=(b,),
        in_specs=[blk, _resident((1, LANES)), _resident((rows, rows))],
        out_specs=blk,
        out_shape=jax.ShapeDtypeStruct((b, s, LANES), F32 if split_scale is None else BF16),
        compiler_params=_params("parallel"),
        name="gate_scan",
    )(z, bias, tri)


def _fox_t_kernel(q_ref, k_ref, cp_ref, vt_ref, o_ref, acc_ref, s_ref, *, tq, tk):
    i = pl.program_id(2)
    hd = A_HEAD_DIM
    ones_row = (hd, 0)

    lane = lax.broadcasted_iota(jnp.int32, (tq, LANES), 1)
    row_v = lax.broadcasted_iota(jnp.int32, (LANES, tk), 0)
    q = q_ref[0].astype(F32)
    q_aug = []
    for h in range(2):
        in_head = (lane >= h * hd) & (lane < (h + 1) * hd)
        head = 2 * pl.program_id(1) + h
        bias_sel = (lane < 3 * A_HEADS) & ((lane & (A_HEADS - 1)) == head)
        q_aug.append(jnp.concatenate(
            [jnp.where(in_head, q, 0.0), jnp.where(bias_sel, -1.0, 0.0)], axis=1).astype(BF16))
    acc_ref[...] = jnp.zeros_like(acc_ref)

    def scores(k0, c0=0):
        kb = jnp.concatenate([k_ref[0, pl.ds(k0, tk), :], cp_ref[0, pl.ds(k0, tk), :]], axis=1)
        return [lax.dot_general(kb, q_aug[h][c0:], (((1,), (1,)), ((), ())),
                                preferred_element_type=F32) for h in range(2)]

    def update(ms, ss, k0, c0=0, masked=False):
        out = []
        for h in range(2):
            s = ss[h]
            if masked:
                row = lax.broadcasted_iota(jnp.int32, (tk, tq - c0), 0)
                col = lax.broadcasted_iota(jnp.int32, (tk, tq - c0), 1)
                s = jnp.where(row <= col, s, -jnp.inf)
            m_old = ms[h][:, c0:]
            m_new = jnp.maximum(m_old, jnp.max(s, axis=0, keepdims=True))
            alpha = jnp.exp2(m_old - m_new)
            p = jnp.exp2(s - m_new).astype(BF16)
            vt = vt_ref[:, pl.ds(k0, tk)]
            vt = jnp.where(row_v == ones_row[h], jnp.ones_like(vt), vt)
            acc_ref[h, :, c0:] = alpha * acc_ref[h, :, c0:] + jnp.dot(
                vt, p, preferred_element_type=F32)
            out.append(m_new if c0 == 0 else jnp.concatenate([ms[h][:, :c0], m_new], axis=1))
        return tuple(out)

    def start(blk):
        return pl.multiple_of(blk * tk, tk)

    def trip(t, ms):
        s_odd = scores(start(2 * t + 1))
        ms = update(ms, [s_ref[h] for h in range(2)], start(2 * t))
        s_next = scores(start(2 * t + 2))
        for h in range(2):
            s_ref[h] = s_next[h]
        return update(ms, s_odd, start(2 * t + 1))

    per_tile = tq // tk
    assert per_tile % 2 == 0
    s_first = scores(start(0))
    for h in range(2):
        s_ref[h] = s_first[h]
    ms = (jnp.full((1, tq), -jnp.inf, F32),) * 2
    ms = lax.fori_loop(0, i * (per_tile // 2), trip, ms)
    first = i * per_tile
    lows = [scores(start(first + jj), c0=jj * tk) for jj in range(1, per_tile)]
    ms = update(ms, [s_ref[h] for h in range(2)], start(first), masked=True)
    for jj in range(1, per_tile):
        ms = update(ms, lows[jj - 1], start(first + jj), c0=jj * tk, masked=True)
    outs = []
    for h in range(2):
        a = acc_ref[h]
        outs.append((a * (1.0 / a[ones_row[h]:ones_row[h] + 1, :])).T)
    o_ref[0] = jnp.where(lane < hd, outs[0], outs[1]).astype(o_ref.dtype)


def _fox_attention_t(qk, cparts, vt, *, tq=2048, tk=256):
    b, s, d2 = qk.shape
    tq = min(tq, s)
    d = d2 // 2
    nblk = d // LANES
    return pl.pallas_call(
        functools.partial(_fox_t_kernel, tq=tq, tk=tk),
        grid=(b, nblk, s // tq),
        in_specs=[pl.BlockSpec((1, tq, LANES), lambda bi, p, i: (bi, i, p)),
                  pl.BlockSpec((1, s, LANES), lambda bi, p, i: (bi, 0, nblk + p)),
                  pl.BlockSpec((1, s, LANES), lambda bi, p, i: (bi, 0, 0)),
                  pl.BlockSpec((LANES, s), lambda bi, p, i: (p, bi))],
        out_specs=pl.BlockSpec((1, tq, LANES), lambda bi, p, i: (bi, i, p)),
        out_shape=jax.ShapeDtypeStruct((b, s, d), BF16),
        scratch_shapes=[pltpu.VMEM((2, LANES, tq), F32), pltpu.VMEM((2, tk, tq), F32)],
        compiler_params=_params("parallel", "parallel", "arbitrary"),
        name="fox_attention",
    )(qk, qk, cparts, vt)


def _mlstm_kernel(q_ref, k_ref, v_ref, o_ref, gcol_ref, grow_ref, cw_ref, cb_ref, ng_ref,
                  out_ref, c_ref, m_ref, tailq_ref, tailk_ref, *, chunk):
    L = chunk
    d = B_HEAD_DIM
    width = B_HEADS * d

    @pl.when(pl.program_id(1) == 0)
    def _():
        c_ref[...] = jnp.zeros_like(c_ref)
        m_ref[...] = jnp.zeros_like(m_ref)
        tailq_ref[0:8] = jnp.zeros((8, width), F32)
        tailk_ref[0:8] = jnp.zeros((8, width), F32)

    tailq_ref[8:8 + L] = q_ref[0].astype(F32)
    tailk_ref[8:8 + L] = k_ref[0].astype(F32)

    def conv_silu(ext_ref, col0, sl):
        y = cb_ref[:, col0 + sl.start:col0 + sl.stop]
        for t in range(B_CONV):
            off = 8 - (B_CONV - 1) + t
            y = y + cw_ref[t:t + 1, col0 + sl.start:col0 + sl.stop] * ext_ref[off:off + L, sl]
        return _silu(y)

    gcol = gcol_ref[0]
    grow = grow_ref[0, 0]
    row = lax.broadcasted_iota(jnp.int32, (L, L), 0)
    col = lax.broadcasted_iota(jnp.int32, (L, L), 1)
    tri = row >= col
    ones = jnp.ones((L, d), BF16)
    wide = lambda t: jnp.concatenate([t] * (L // d), axis=1)

    for h in range(B_HEADS):
        sl = slice(h * d, (h + 1) * d)
        qh = conv_silu(tailq_ref, 0, sl).astype(BF16)
        kf = conv_silu(tailk_ref, width, sl) * (d ** -0.5)
        kh = kf.astype(BF16)
        vaug = jnp.concatenate([v_ref[0, :, sl], ones], axis=1)
        bc = jnp.broadcast_to(gcol[:, h:h + 1], (L, d))
        ig = jnp.broadcast_to(gcol[:, B_HEADS + h:B_HEADS + h + 1], (L, d))
        br = grow[h:h + 1, :]
        igr = grow[B_HEADS + h:B_HEADS + h + 1, :]
        m_old = m_ref[h:h + 1, :]

        log_d = jnp.where(tri, wide(bc) - br + igr, -jnp.inf)
        m_inter = bc + m_old
        m_t = jnp.maximum(m_inter, jnp.max(log_d, axis=-1, keepdims=True))
        dm = jnp.exp(log_d - wide(m_t))
        a = jnp.exp(m_inter - m_t)
        sqk = lax.dot_general(qh, kh, (((1,), (1,)), ((), ())), preferred_element_type=F32) * dm
        c_aug = c_ref[h]
        qc = jnp.dot(qh, c_aug.astype(BF16), preferred_element_type=F32)
        sv = jnp.dot(sqk.astype(BF16), vaug, preferred_element_type=F32)
        num = a * qc[:, :d] + sv[:, :d]
        den = a * qc[:, d:] + sv[:, d:]
        hc = num * (1.0 / jnp.maximum(jnp.abs(den), jnp.exp(-m_t)))
        hc = hc * lax.rsqrt(jnp.mean(hc * hc, axis=-1, keepdims=True) + EPS)
        gate = _sigmoid(o_ref[0, :, sl].astype(F32))
        out_ref[0, :, sl] = (hc * ng_ref[:, sl] * gate).astype(out_ref.dtype)

        b_last = bc[L - 1:L, :]
        g = b_last - bc + ig
        m_new = jnp.maximum(b_last + m_old, jnp.max(g, axis=0, keepdims=True))
        w = jnp.exp(g - m_new)
        decay = jnp.exp(b_last + m_old - m_new)
        kw = (kf * w).astype(BF16)
        upd = lax.dot_general(kw, vaug, (((0,), (0,)), ((), ())), preferred_element_type=F32)
        c_ref[h] = jnp.concatenate([decay, decay], axis=1) * c_aug + upd
        m_ref[h:h + 1, :] = m_new

    tailq_ref[0:8] = tailq_ref[L:L + 8]
    tailk_ref[0:8] = tailk_ref[L:L + 8]


def _mlstm(proj, gcol, grow, conv_w, conv_b, norm_g, *, chunk):
    b, s, d4 = proj.shape
    d = d4 // 4
    nc = s // chunk
    blk = lambda c0: pl.BlockSpec((1, chunk, d), lambda bi, c: (bi, c, c0))
    return pl.pallas_call(
        functools.partial(_mlstm_kernel, chunk=chunk),
        grid=(b, nc),
        in_specs=[blk(0), blk(1), blk(2), blk(3),
                  pl.BlockSpec((1, chunk, LANES), lambda bi, c: (bi, c, 0)),
                  pl.BlockSpec((1, 1, 2 * B_HEADS, chunk), lambda bi, c: (bi, c, 0, 0)),
                  _resident((B_CONV, 2 * d)), _resident((1, 2 * d)), _resident((1, d))],
        out_specs=pl.BlockSpec((1, chunk, d), lambda bi, c: (bi, c, 0)),
        out_shape=jax.ShapeDtypeStruct((b, s, d), BF16),
        scratch_shapes=[pltpu.VMEM((B_HEADS, B_HEAD_DIM, 2 * B_HEAD_DIM), F32),
                        pltpu.VMEM((B_HEADS, LANES), F32),
                        pltpu.VMEM((8 + chunk, d), F32), pltpu.VMEM((8 + chunk, d), F32)],
        compiler_params=_params("parallel", "arbitrary"),
        name="mlstm",
    )(proj, proj, proj, proj, gcol, grow, conv_w, conv_b.reshape(1, 2 * d), norm_g.reshape(1, d))


def _gmlp_kernel(x_ref, g_ref, wu_ref, wv_ref, lng_ref, lnb_ref, ws_ref, bs_ref, o_ref, v_ref):
    tm = x_ref.shape[0]
    width = wv_ref.shape[1]
    gw = width // C_GROUPS
    hn = _rmsnorm(x_ref[...], g_ref[...]).astype(BF16)
    for c0 in range(0, width, gw):
        v_ref[:, c0:c0 + gw] = _gelu_exact(
            jnp.dot(hn, wv_ref[:, c0:c0 + gw], preferred_element_type=F32))
    v = v_ref[...]
    mu = jnp.mean(v, axis=-1, keepdims=True)
    vc = v - mu
    var = jnp.mean(vc * vc, axis=-1, keepdims=True)
    vn = (vc * lax.rsqrt(var + EPS) * lng_ref[...] + lnb_ref[...]).astype(BF16)
    row = lax.broadcasted_iota(jnp.int32, (C_SPAN, C_SPAN), 0)
    col = lax.broadcasted_iota(jnp.int32, (C_SPAN, C_SPAN), 1)
    mask = (row // CHUNK) >= (col // CHUNK)
    for gi in range(C_GROUPS):
        cs = slice(gi * gw, (gi + 1) * gw)
        ws = jnp.where(mask, ws_ref[gi], 0.0).astype(BF16)
        bias = bs_ref[:, gi:gi + 1]
        u = _gelu_exact(jnp.dot(hn, wu_ref[:, cs], preferred_element_type=F32))
        for r0 in range(0, tm, C_SPAN):
            s = jnp.dot(ws, vn[r0:r0 + C_SPAN, cs], preferred_element_type=F32) + bias
            o_ref[r0:r0 + C_SPAN, cs] = (u[r0:r0 + C_SPAN] * s).astype(o_ref.dtype)


def _gmlp(x2, g, w_in_all, layer, ln_g, ln_b, w_s, bs_t, *, tm=512):
    t, d = x2.shape
    width = w_in_all.shape[2] // 2
    wu = wv = w_in_all
    return pl.pallas_call(
        _gmlp_kernel,
        grid=(t // tm,),
        in_specs=[pl.BlockSpec((tm, d), lambda i: (i, 0)), _resident((1, d)),
                  _layer_block((d, width), layer, 0), _layer_block((d, width), layer, 1),
                  _resident((1, width)), _resident((1, width)),
                  _resident((C_GROUPS, C_SPAN, C_SPAN)), _resident((C_SPAN, LANES))],
        out_specs=pl.BlockSpec((tm, width), lambda i: (i, 0)),
        out_shape=jax.ShapeDtypeStruct((t, width), BF16),
        scratch_shapes=[pltpu.VMEM((tm, width), F32)],
        compiler_params=_params("parallel"),
        name="gmlp",
    )(x2, g.reshape(1, d), wu, wv, ln_g.reshape(1, width), ln_b.reshape(1, width), w_s, bs_t)


def _pad_lanes(a):
    return jnp.pad(a, [(0, 0)] * (a.ndim - 1) + [(0, LANES - a.shape[-1])])


def _mixer_a(x2, g, w_in_all, w_in_f32, j, b_f, b, s):
    d = x2.shape[1]
    wvt = w_in_f32[j, :, 2 * d:3 * d].T.astype(BF16)
    w_f = jnp.tile(w_in_f32[j, :, 3 * d:], (1, 3))
    qk, vt, zf = _norm_proj_a(x2, g, w_in_all, j, 2 * d, wvt, _pad_lanes(w_f).astype(BF16),
                              lead_cols=d, lead_scale=A_HEAD_DIM ** -0.5 * LOG2E)
    cparts = _gate_scan(zf.reshape(b, s, LANES), _pad_lanes(jnp.tile(b_f, 3)[None, :]),
                        n_forget=3 * A_HEADS, segment=s, split_scale=LOG2E)
    return _fox_attention_t(qk.reshape(b, s, 2 * d), cparts, vt).reshape(b * s, d)


def _mixer_b(x2, g, w_in_all, w_in_f32, j, conv_w, conv_b, b_i, b_f, norm_g, b, s):
    d = x2.shape[1]
    chunk = MLSTM_CHUNK
    w_gate = jnp.concatenate([w_in_f32[j, :, 4 * d + B_HEADS:], w_in_f32[j, :, 4 * d:4 * d + B_HEADS]],
                             axis=1)
    proj, zg = _norm_proj(x2, g, w_in_all, j, 4 * d, _pad_lanes(w_gate).astype(BF16))
    bias = _pad_lanes(jnp.concatenate([b_f, b_i])[None, :])
    gcol = _gate_scan(zg.reshape(b, s, LANES), bias, n_forget=B_HEADS, segment=chunk)
    grow = gcol[..., :2 * B_HEADS].reshape(b, s // chunk, chunk, 2 * B_HEADS).transpose(0, 1, 3, 2)
    hs = _mlstm(proj.reshape(b, s, 4 * d), gcol, grow, conv_w, conv_b, norm_g, chunk=chunk)
    return hs.reshape(b * s, d)


def kernel(x, norm1_g, norm2_g, final_g, a_w_in, a_b_f, a_w_out, b_w_in, b_conv_w, b_conv_b,
           b_b_i, b_b_f, b_norm_g, b_w_out, c_w_in, c_ln_g, c_ln_b, c_w_s, c_b_s, c_w_out,
           ffn_w_gu, ffn_w_down):
    b, s, d = x.shape
    depth = norm1_g.shape[0]
    a_in, b_in, c_in = a_w_in.astype(BF16), b_w_in.astype(BF16), c_w_in.astype(BF16)
    w_outs = (a_w_out.astype(BF16), b_w_out.astype(BF16), c_w_out.astype(BF16))
    w_gu, w_down = ffn_w_gu.astype(BF16), ffn_w_down.astype(BF16)
    x2 = x.reshape(b * s, d)
    for i in range(depth):
        kind, j = i % 3, i // 3
        if kind == 0:
            a = _mixer_a(x2, norm1_g[i], a_in, a_w_in, j, a_b_f[j], b, s)
        elif kind == 1:
            a = _mixer_b(x2, norm1_g[i], b_in, b_w_in, j, b_conv_w[j], b_conv_b[j], b_b_i[j],
                         b_b_f[j], b_norm_g[j], b, s)
        else:
            a = _gmlp(x2, norm1_g[i], c_in, j, c_ln_g[j], c_ln_b[j], c_w_s[j], _pad_lanes(c_b_s[j].T))
        x2 = _out_ffn(a, w_outs[kind], j, x2, norm2_g[i], w_gu, w_down, i, final_g,
                      final_norm=(i == depth - 1))
    return x2.reshape(b, s, d)
```

```python
import functools
import math

import jax
import jax.numpy as jnp
from jax import lax
from jax.experimental import pallas as pl
from jax.experimental.pallas import tpu as pltpu

F32 = jnp.float32
BF16 = jnp.bfloat16

EPS = 1e-6
LOG2E = math.log2(math.e)
LANES = 128
A_HEADS = 16
A_HEAD_DIM = 64
B_HEADS = 8
B_HEAD_DIM = 128
B_CONV = 4
C_GROUPS = 8
C_SPAN = 128
CHUNK = 64
MLSTM_CHUNK = 256
VMEM_LIMIT = 56 * 1024 * 1024


def _params(*sem, flags=None):
    return pltpu.CompilerParams(dimension_semantics=sem, vmem_limit_bytes=VMEM_LIMIT, flags=flags)


def _resident(shape):
    nd = len(shape)
    return pl.BlockSpec(shape, lambda *_: (0,) * nd, pipeline_mode=pl.Buffered(1))


def _layer_block(shape, layer, col_block=0):
    return pl.BlockSpec((None,) + tuple(shape), lambda *_: (layer, 0, col_block),
                        pipeline_mode=pl.Buffered(1))


def _rmsnorm(x, g):
    ms = jnp.mean(x * x, axis=-1, keepdims=True)
    return x * lax.rsqrt(ms + EPS) * g


def _sigmoid(x):
    return 0.5 + 0.5 * jnp.tanh(0.5 * x)


def _silu(x):
    h = 0.5 * x
    return h + h * jnp.tanh(h)


def _log_sigmoid(z):
    return jnp.minimum(z, 0.0) - jnp.log1p(jnp.exp(-jnp.abs(z)))


def _gelu_exact(x):
    return 0.5 * x * (1.0 + lax.erf(x * (1.0 / math.sqrt(2.0))))


def _norm_proj_kernel(x_ref, g_ref, w_ref, wg_ref, o_ref, og_ref, *, col_chunk):
    n = w_ref.shape[1]
    tm = x_ref.shape[0]
    half = tm // 2
    for r0 in range(0, tm, half):
        rs = slice(r0, r0 + half)
        hn = _rmsnorm(x_ref[rs, :], g_ref[...]).astype(BF16)
        for c0 in range(0, n, col_chunk):
            y = jnp.dot(hn, w_ref[:, c0:c0 + col_chunk], preferred_element_type=F32)
            o_ref[rs, c0:c0 + col_chunk] = y.astype(o_ref.dtype)
        og_ref[rs, :] = jnp.dot(hn, wg_ref[...], preferred_element_type=F32)


def _norm_proj(x2, g, w_all, layer, n, wg, *, tm=512, col_chunk=512):
    t, d = x2.shape
    return pl.pallas_call(
        functools.partial(_norm_proj_kernel, col_chunk=col_chunk),
        grid=(t // tm,),
        in_specs=[pl.BlockSpec((tm, d), lambda i: (i, 0)),
                  _resident((1, d)), _layer_block((d, n), layer), _resident((d, LANES))],
        out_specs=[pl.BlockSpec((tm, n), lambda i: (i, 0)),
                   pl.BlockSpec((tm, LANES), lambda i: (i, 0))],
        out_shape=[jax.ShapeDtypeStruct((t, n), BF16),
                   jax.ShapeDtypeStruct((t, LANES), F32)],
        compiler_params=_params("parallel"),
        name="norm_proj",
    )(x2, g.reshape(1, d), w_all, wg)


def _norm_proj_a_kernel(x_ref, g_ref, w_ref, wvt_ref, wg_ref, o_ref, vt_ref, og_ref, *, col_chunk,
                        lead_cols, lead_scale):
    n = w_ref.shape[1]
    d_v = wvt_ref.shape[0]
    tm = x_ref.shape[0]
    half = tm // 2
    for r0 in range(0, tm, half):
        rs = slice(r0, r0 + half)
        hn = _rmsnorm(x_ref[rs, :], g_ref[...]).astype(BF16)
        for c0 in range(0, n, col_chunk):
            y = jnp.dot(hn, w_ref[:, c0:c0 + col_chunk], preferred_element_type=F32)
            if c0 + col_chunk <= lead_cols:
                y = y * lead_scale
            o_ref[rs, c0:c0 + col_chunk] = y.astype(o_ref.dtype)
        for c0 in range(0, d_v, col_chunk):
            yt = lax.dot_general(wvt_ref[c0:c0 + col_chunk, :], hn, (((1,), (1,)), ((), ())),
                                 preferred_element_type=F32)
            vt_ref[c0:c0 + col_chunk, rs] = yt.astype(vt_ref.dtype)
        og_ref[rs, :] = jnp.dot(hn, wg_ref[...], preferred_element_type=F32)


def _norm_proj_a(x2, g, w_all, layer, n, wvt, wg, *, lead_cols, lead_scale, tm=512, col_chunk=512):
    t, d = x2.shape
    d_v = wvt.shape[0]
    return pl.pallas_call(
        functools.partial(_norm_proj_a_kernel, col_chunk=col_chunk, lead_cols=lead_cols,
                          lead_scale=lead_scale),
        grid=(t // tm,),
        in_specs=[pl.BlockSpec((tm, d), lambda i: (i, 0)),
                  _resident((1, d)), _layer_block((d, n), layer), _resident((d_v, d)),
                  _resident((d, LANES))],
        out_specs=[pl.BlockSpec((tm, n), lambda i: (i, 0)),
                   pl.BlockSpec((d_v, tm), lambda i: (0, i)),
                   pl.BlockSpec((tm, LANES), lambda i: (i, 0))],
        out_shape=[jax.ShapeDtypeStruct((t, n), BF16),
                   jax.ShapeDtypeStruct((d_v, t), BF16),
                   jax.ShapeDtypeStruct((t, LANES), F32)],
        compiler_params=_params("parallel"),
        name="norm_proj_a",
    )(x2, g.reshape(1, d), w_all, wvt, wg)


def _out_ffn_kernel(a_ref, wo_ref, x_ref, g_ref, wg_ref, wu_ref, wd_ref, fg_ref, o_ref, *,
                    ff_chunk, final_norm):
    x1 = x_ref[...] + jnp.dot(a_ref[...], wo_ref[...], preferred_element_type=F32)
    hn = _rmsnorm(x1, g_ref[...]).astype(BF16)
    d_ff = wg_ref.shape[1]
    acc = x1
    for c0 in range(0, d_ff, ff_chunk):
        gate = jnp.dot(hn, wg_ref[:, c0:c0 + ff_chunk], preferred_element_type=F32)
        up = jnp.dot(hn, wu_ref[:, c0:c0 + ff_chunk], preferred_element_type=F32)
        h = (_silu(gate) * up).astype(BF16)
        acc = acc + jnp.dot(h, wd_ref[c0:c0 + ff_chunk, :], preferred_element_type=F32)
    if final_norm:
        acc = _rmsnorm(acc, fg_ref[...])
    o_ref[...] = acc


def _out_ffn(a, wo_all, wo_layer, x2, g, w_gu, w_down, layer, final_g, *, final_norm, tm=512,
             ff_chunk=256):
    t, d = x2.shape
    k = a.shape[1]
    d_ff = w_down.shape[1]
    return pl.pallas_call(
        functools.partial(_out_ffn_kernel, ff_chunk=ff_chunk, final_norm=final_norm),
        grid=(t // tm,),
        in_specs=[pl.BlockSpec((tm, k), lambda i: (i, 0)), _layer_block((k, d), wo_layer),
                  pl.BlockSpec((tm, d), lambda i: (i, 0)), _resident((1, d)),
                  _layer_block((d, d_ff), layer, 0), _layer_block((d, d_ff), layer, 1),
                  _layer_block((d_ff, d), layer), _resident((1, d))],
        out_specs=pl.BlockSpec((tm, d), lambda i: (i, 0)),
        out_shape=jax.ShapeDtypeStruct((t, d), F32),
        compiler_params=_params("parallel"),
        name="out_ffn",
    )(a, wo_all, x2, g.reshape(1, d), w_gu, w_gu, w_down, final_g.reshape(1, d))


def _gate_scan_kernel(z_ref, bias_ref, tri_ref, o_ref, *, n_forget, segment, split_scale):
    rows = tri_ref.shape[0]
    lane = lax.broadcasted_iota(jnp.int32, (rows, LANES), 1)
    forget = lane < n_forget
    tri = tri_ref[...]
    run = None
    for r0 in range(0, z_ref.shape[1], rows):
        z = z_ref[0, r0:r0 + rows] + bias_ref[...]
        lf = jnp.where(forget, _log_sigmoid(z), 0.0)
        cum = jnp.zeros_like(lf)
        rest = lf
        for _ in range(3):
            part = rest.astype(BF16)
            cum = cum + jnp.dot(tri, part, preferred_element_type=F32)
            rest = rest - part.astype(F32)
        if segment > rows:
            if r0 % segment != 0:
                cum = cum + run
            run = cum[rows - 1:rows, :]
        rs = slice(r0, r0 + rows)
        if split_scale is None:
            o_ref[0, rs] = jnp.where(forget, cum, z)
        else:
            group = n_forget // 3
            val = cum * split_scale
            hi = val.astype(BF16).astype(F32)
            mid = (val - hi).astype(BF16).astype(F32)
            lo = val - hi - mid
            o_ref[0, rs] = jnp.where(lane < group, hi,
                                     jnp.where(lane < 2 * group, mid, lo)).astype(o_ref.dtype)


def _gate_scan(z, bias, *, n_forget, segment, split_scale=None):
    b, s, _ = z.shape
    rows = C_SPAN
    r = jnp.arange(rows)
    seg = min(segment, rows)
    tri = ((r[:, None] >= r[None, :]) & (r[:, None] // seg == r[None, :] // seg)).astype(BF16)
    blk = pl.BlockSpec((1, s, LANES), lambda i: (i, 0, 0))
    return pl.pallas_call(
        functools.partial(_gate_scan_kernel, n_forget=n_forget, segment=segment,
                          split_scale=split_scale),
        grid=(b,),
        in_specs=[blk, _resident((1, LANES)), _resident((rows, rows))],
        out_specs=blk,
        out_shape=jax.ShapeDtypeStruct((b, s, LANES), F32 if split_scale is None else BF16),
        compiler_params=_params("parallel"),
        name="gate_scan",
    )(z, bias, tri)


def _fox_t_kernel(q_ref, k_ref, cp_ref, vt_ref, o_ref, acc_ref, s_ref, *, tq, tk):
    i = pl.program_id(2)
    hd = A_HEAD_DIM
    ones_row = (hd, 0)

    lane = lax.broadcasted_iota(jnp.int32, (tq, LANES), 1)
    row_v = lax.broadcasted_iota(jnp.int32, (LANES, tk), 0)
    q = q_ref[0].astype(F32)
    q_aug = []
    for h in range(2):
        in_head = (lane >= h * hd) & (lane < (h + 1) * hd)
        head = 2 * pl.program_id(1) + h
        bias_sel = (lane < 3 * A_HEADS) & ((lane & (A_HEADS - 1)) == head)
        q_aug.append(jnp.concatenate(
            [jnp.where(in_head, q, 0.0), jnp.where(bias_sel, -1.0, 0.0)], axis=1).astype(BF16))
    acc_ref[...] = jnp.zeros_like(acc_ref)

    def scores(k0, c0=0):
        kb = jnp.concatenate([k_ref[0, pl.ds(k0, tk), :], cp_ref[0, pl.ds(k0, tk), :]], axis=1)
        return [lax.dot_general(kb, q_aug[h][c0:], (((1,), (1,)), ((), ())),
                                preferred_element_type=F32) for h in range(2)]

    def update(ms, ss, k0, c0=0, masked=False):
        out = []
        for h in range(2):
            s = ss[h]
            if masked:
                row = lax.broadcasted_iota(jnp.int32, (tk, tq - c0), 0)
                col = lax.broadcasted_iota(jnp.int32, (tk, tq - c0), 1)
                s = jnp.where(row <= col, s, -jnp.inf)
            m_old = ms[h][:, c0:]
            m_new = jnp.maximum(m_old, jnp.max(s, axis=0, keepdims=True))
            alpha = jnp.exp2(m_old - m_new)
            p = jnp.exp2(s - m_new).astype(BF16)
            vt = vt_ref[:, pl.ds(k0, tk)]
            vt = jnp.where(row_v == ones_row[h], jnp.ones_like(vt), vt)
            acc_ref[h, :, c0:] = alpha * acc_ref[h, :, c0:] + jnp.dot(
                vt, p, preferred_element_type=F32)
            out.append(m_new if c0 == 0 else jnp.concatenate([ms[h][:, :c0], m_new], axis=1))
        return tuple(out)

    def start(blk):
        return pl.multiple_of(blk * tk, tk)

    def trip(t, ms):
        s_odd = scores(start(2 * t + 1))
        ms = update(ms, [s_ref[h] for h in range(2)], start(2 * t))
        s_next = scores(start(2 * t + 2))
        for h in range(2):
            s_ref[h] = s_next[h]
        return update(ms, s_odd, start(2 * t + 1))

    per_tile = tq // tk
    assert per_tile % 2 == 0
    s_first = scores(start(0))
    for h in range(2):
        s_ref[h] = s_first[h]
    ms = (jnp.full((1, tq), -jnp.inf, F32),) * 2
    ms = lax.fori_loop(0, i * (per_tile // 2), trip, ms)
    first = i * per_tile
    lows = [scores(start(first + jj), c0=jj * tk) for jj in range(1, per_tile)]
    ms = update(ms, [s_ref[h] for h in range(2)], start(first), masked=True)
    for jj in range(1, per_tile):
        ms = update(ms, lows[jj - 1], start(first + jj), c0=jj * tk, masked=True)
    outs = []
    for h in range(2):
        a = acc_ref[h]
        outs.append((a * (1.0 / a[ones_row[h]:ones_row[h] + 1, :])).T)
    o_ref[0] = jnp.where(lane < hd, outs[0], outs[1]).astype(o_ref.dtype)


def _fox_attention_t(qk, cparts, vt, *, tq=2048, tk=256):
    b, s, d2 = qk.shape
    tq = min(tq, s)
    d = d2 // 2
    nblk = d // LANES
    return pl.pallas_call(
        functools.partial(_fox_t_kernel, tq=tq, tk=tk),
        grid=(b, nblk, s // tq),
        in_specs=[pl.BlockSpec((1, tq, LANES), lambda bi, p, i: (bi, i, p)),
                  pl.BlockSpec((1, s, LANES), lambda bi, p, i: (bi, 0, nblk + p)),
                  pl.BlockSpec((1, s, LANES), lambda bi, p, i: (bi, 0, 0)),
                  pl.BlockSpec((LANES, s), lambda bi, p, i: (p, bi))],
        out_specs=pl.BlockSpec((1, tq, LANES), lambda bi, p, i: (bi, i, p)),
        out_shape=jax.ShapeDtypeStruct((b, s, d), BF16),
        scratch_shapes=[pltpu.VMEM((2, LANES, tq), F32), pltpu.VMEM((2, tk, tq), F32)],
        compiler_params=_params("parallel", "parallel", "arbitrary"),
        name="fox_attention",
    )(qk, qk, cparts, vt)


def _mlstm_kernel(q_ref, k_ref, v_ref, o_ref, gcol_ref, grow_ref, cw_ref, cb_ref, ng_ref, sh_ref,
                  out_ref, c_ref, m_ref, tailq_ref, tailk_ref, shq_ref, shk_ref, *, chunk):
    L = chunk
    d = B_HEAD_DIM
    width = B_HEADS * d
    taps = B_CONV - 1

    @pl.when(pl.program_id(1) == 0)
    def _():
        c_ref[...] = jnp.zeros_like(c_ref)
        m_ref[...] = jnp.zeros_like(m_ref)
        tailq_ref[...] = jnp.zeros_like(tailq_ref)
        tailk_ref[...] = jnp.zeros_like(tailk_ref)

    shq_ref[...] = jnp.dot(sh_ref[...], q_ref[0], preferred_element_type=F32)
    shk_ref[...] = jnp.dot(sh_ref[...], k_ref[0], preferred_element_type=F32)

    def conv_silu(x_ref, sh_x_ref, tail_ref, col0, sl):
        cs = slice(col0 + sl.start, col0 + sl.stop)
        x = x_ref[0, :, sl].astype(F32)
        y = cb_ref[:, cs] + cw_ref[taps:taps + 1, cs] * x
        for j in range(taps):
            y = y + cw_ref[taps - 1 - j:taps - j, cs] * sh_x_ref[j * L:(j + 1) * L, sl]
        slab = jnp.concatenate([tail_ref[:, sl], jnp.zeros((8, d), F32)], axis=0)
        head = y[0:8]
        for j in range(taps):
            head = head + cw_ref[taps - 1 - j:taps - j, cs] * slab[8 - (j + 1):16 - (j + 1)]
        return _silu(jnp.concatenate([head, y[8:]], axis=0))

    gcol = gcol_ref[0]
    grow = grow_ref[0, 0]
    row = lax.broadcasted_iota(jnp.int32, (L, L), 0)
    col = lax.broadcasted_iota(jnp.int32, (L, L), 1)
    tri = row >= col
    ones = jnp.ones((L, d), BF16)
    wide = lambda t: jnp.concatenate([t] * (L // d), axis=1)

    for h in range(B_HEADS):
        sl = slice(h * d, (h + 1) * d)
        qh = conv_silu(q_ref, shq_ref, tailq_ref, 0, sl).astype(BF16)
        kf = conv_silu(k_ref, shk_ref, tailk_ref, width, sl) * (d ** -0.5)
        kh = kf.astype(BF16)
        vaug = jnp.concatenate([v_ref[0, :, sl], ones], axis=1)
        bc = jnp.broadcast_to(gcol[:, h:h + 1], (L, d))
        ig = jnp.broadcast_to(gcol[:, B_HEADS + h:B_HEADS + h + 1], (L, d))
        br = grow[h:h + 1, :]
        igr = grow[B_HEADS + h:B_HEADS + h + 1, :]
        m_old = m_ref[h:h + 1, :]

        log_d = jnp.where(tri, wide(bc) - br + igr, -jnp.inf)
        m_inter = bc + m_old
        m_t = jnp.maximum(m_inter, jnp.max(log_d, axis=-1, keepdims=True))
        dm = jnp.exp(log_d - wide(m_t))
        a = jnp.exp(m_inter - m_t)
        sqk = lax.dot_general(qh, kh, (((1,), (1,)), ((), ())), preferred_element_type=F32) * dm
        c_aug = c_ref[h]
        qc = jnp.dot(qh, c_aug.astype(BF16), preferred_element_type=F32)
        sv = jnp.dot(sqk.astype(BF16), vaug, preferred_element_type=F32)
        num = a * qc[:, :d] + sv[:, :d]
        den = a * qc[:, d:] + sv[:, d:]
        hc = num * (1.0 / jnp.maximum(jnp.abs(den), jnp.exp(-m_t)))
        hc = hc * lax.rsqrt(jnp.mean(hc * hc, axis=-1, keepdims=True) + EPS)
        gate = _sigmoid(o_ref[0, :, sl].astype(F32))
        out_ref[0, :, sl] = (hc * ng_ref[:, sl] * gate).astype(out_ref.dtype)

        b_last = bc[L - 1:L, :]
        g = b_last - bc + ig
        m_new = jnp.maximum(b_last + m_old, jnp.max(g, axis=0, keepdims=True))
        w = jnp.exp(g - m_new)
        decay = jnp.exp(b_last + m_old - m_new)
        kw = (kf * w).astype(BF16)
        upd = lax.dot_general(kw, vaug, (((0,), (0,)), ((), ())), preferred_element_type=F32)
        c_ref[h] = jnp.concatenate([decay, decay], axis=1) * c_aug + upd
        m_ref[h:h + 1, :] = m_new

    tailq_ref[...] = q_ref[0, L - 8:L, :].astype(F32)
    tailk_ref[...] = k_ref[0, L - 8:L, :].astype(F32)


def _mlstm(proj, gcol, grow, conv_w, conv_b, norm_g, *, chunk):
    b, s, d4 = proj.shape
    d = d4 // 4
    nc = s // chunk
    blk = lambda c0: pl.BlockSpec((1, chunk, d), lambda bi, c: (bi, c, c0))
    taps = B_CONV - 1
    r = jnp.arange(taps * chunk)
    shift = (r[:, None] % chunk - r[:, None] // chunk - 1 == jnp.arange(chunk)[None, :]).astype(BF16)
    return pl.pallas_call(
        functools.partial(_mlstm_kernel, chunk=chunk),
        grid=(b, nc),
        in_specs=[blk(0), blk(1), blk(2), blk(3),
                  pl.BlockSpec((1, chunk, LANES), lambda bi, c: (bi, c, 0)),
                  pl.BlockSpec((1, 1, 2 * B_HEADS, chunk), lambda bi, c: (bi, c, 0, 0)),
                  _resident((B_CONV, 2 * d)), _resident((1, 2 * d)), _resident((1, d)),
                  _resident((taps * chunk, chunk))],
        out_specs=pl.BlockSpec((1, chunk, d), lambda bi, c: (bi, c, 0)),
        out_shape=jax.ShapeDtypeStruct((b, s, d), BF16),
        scratch_shapes=[pltpu.VMEM((B_HEADS, B_HEAD_DIM, 2 * B_HEAD_DIM), F32),
                        pltpu.VMEM((B_HEADS, LANES), F32),
                        pltpu.VMEM((8, d), F32), pltpu.VMEM((8, d), F32),
                        pltpu.VMEM((taps * chunk, d), F32), pltpu.VMEM((taps * chunk, d), F32)],
        compiler_params=_params("parallel", "arbitrary"),
        name="mlstm",
    )(proj, proj, proj, proj, gcol, grow, conv_w, conv_b.reshape(1, 2 * d), norm_g.reshape(1, d),
      shift)


def _gmlp_kernel(x_ref, g_ref, wu_ref, wv_ref, lng_ref, lnb_ref, ws_ref, bs_ref, o_ref, v_ref):
    tm = x_ref.shape[0]
    width = wv_ref.shape[1]
    gw = width // C_GROUPS
    hn = _rmsnorm(x_ref[...], g_ref[...]).astype(BF16)
    for c0 in range(0, width, gw):
        v_ref[:, c0:c0 + gw] = _gelu_exact(
            jnp.dot(hn, wv_ref[:, c0:c0 + gw], preferred_element_type=F32))
    v = v_ref[...]
    mu = jnp.mean(v, axis=-1, keepdims=True)
    vc = v - mu
    var = jnp.mean(vc * vc, axis=-1, keepdims=True)
    vn = (vc * lax.rsqrt(var + EPS) * lng_ref[...] + lnb_ref[...]).astype(BF16)
    row = lax.broadcasted_iota(jnp.int32, (C_SPAN, C_SPAN), 0)
    col = lax.broadcasted_iota(jnp.int32, (C_SPAN, C_SPAN), 1)
    mask = (row // CHUNK) >= (col // CHUNK)
    for gi in range(C_GROUPS):
        cs = slice(gi * gw, (gi + 1) * gw)
        ws = jnp.where(mask, ws_ref[gi], 0.0).astype(BF16)
        bias = bs_ref[:, gi:gi + 1]
        u = _gelu_exact(jnp.dot(hn, wu_ref[:, cs], preferred_element_type=F32))
        for r0 in range(0, tm, C_SPAN):
            s = jnp.dot(ws, vn[r0:r0 + C_SPAN, cs], preferred_element_type=F32) + bias
            o_ref[r0:r0 + C_SPAN, cs] = (u[r0:r0 + C_SPAN] * s).astype(o_ref.dtype)


def _gmlp(x2, g, w_in_all, layer, ln_g, ln_b, w_s, bs_t, *, tm=512):
    t, d = x2.shape
    width = w_in_all.shape[2] // 2
    wu = wv = w_in_all
    return pl.pallas_call(
        _gmlp_kernel,
        grid=(t // tm,),
        in_specs=[pl.BlockSpec((tm, d), lambda i: (i, 0)), _resident((1, d)),
                  _layer_block((d, width), layer, 0), _layer_block((d, width), layer, 1),
                  _resident((1, width)), _resident((1, width)),
                  _resident((C_GROUPS, C_SPAN, C_SPAN)), _resident((C_SPAN, LANES))],
        out_specs=pl.BlockSpec((tm, width), lambda i: (i, 0)),
        out_shape=jax.ShapeDtypeStruct((t, width), BF16),
        scratch_shapes=[pltpu.VMEM((tm, width), F32)],
        compiler_params=_params("parallel"),
        name="gmlp",
    )(x2, g.reshape(1, d), wu, wv, ln_g.reshape(1, width), ln_b.reshape(1, width), w_s, bs_t)


def _pad_lanes(a):
    return jnp.pad(a, [(0, 0)] * (a.ndim - 1) + [(0, LANES - a.shape[-1])])


def _mixer_a(x2, g, w_in_all, w_in_f32, j, b_f, b, s):
    d = x2.shape[1]
    wvt = w_in_f32[j, :, 2 * d:3 * d].T.astype(BF16)
    w_f = jnp.tile(w_in_f32[j, :, 3 * d:], (1, 3))
    qk, vt, zf = _norm_proj_a(x2, g, w_in_all, j, 2 * d, wvt, _pad_lanes(w_f).astype(BF16),
                              lead_cols=d, lead_scale=A_HEAD_DIM ** -0.5 * LOG2E)
    cparts = _gate_scan(zf.reshape(b, s, LANES), _pad_lanes(jnp.tile(b_f, 3)[None, :]),
                        n_forget=3 * A_HEADS, segment=s, split_scale=LOG2E)
    return _fox_attention_t(qk.reshape(b, s, 2 * d), cparts, vt).reshape(b * s, d)


def _mixer_b(x2, g, w_in_all, w_in_f32, j, conv_w, conv_b, b_i, b_f, norm_g, b, s):
    d = x2.shape[1]
    chunk = MLSTM_CHUNK
    w_gate = jnp.concatenate([w_in_f32[j, :, 4 * d + B_HEADS:], w_in_f32[j, :, 4 * d:4 * d + B_HEADS]],
                             axis=1)
    proj, zg = _norm_proj(x2, g, w_in_all, j, 4 * d, _pad_lanes(w_gate).astype(BF16))
    bias = _pad_lanes(jnp.concatenate([b_f, b_i])[None, :])
    gcol = _gate_scan(zg.reshape(b, s, LANES), bias, n_forget=B_HEADS, segment=chunk)
    grow = gcol[..., :2 * B_HEADS].reshape(b, s // chunk, chunk, 2 * B_HEADS).transpose(0, 1, 3, 2)
    hs = _mlstm(proj.reshape(b, s, 4 * d), gcol, grow, conv_w, conv_b, norm_g, chunk=chunk)
    return hs.reshape(b * s, d)


def kernel(x, norm1_g, norm2_g, final_g, a_w_in, a_b_f, a_w_out, b_w_in, b_conv_w, b_conv_b,
           b_b_i, b_b_f, b_norm_g, b_w_out, c_w_in, c_ln_g, c_ln_b, c_w_s, c_b_s, c_w_out,
           ffn_w_gu, ffn_w_down):
    b, s, d = x.shape
    depth = norm1_g.shape[0]
    a_in, b_in, c_in = a_w_in.astype(BF16), b_w_in.astype(BF16), c_w_in.astype(BF16)
    w_outs = (a_w_out.astype(BF16), b_w_out.astype(BF16), c_w_out.astype(BF16))
    w_gu, w_down = ffn_w_gu.astype(BF16), ffn_w_down.astype(BF16)
    x2 = x.reshape(b * s, d)
    for i in range(depth):
        kind, j = i % 3, i // 3
        if kind == 0:
            a = _mixer_a(x2, norm1_g[i], a_in, a_w_in, j, a_b_f[j], b, s)
        elif kind == 1:
            a = _mixer_b(x2, norm1_g[i], b_in, b_w_in, j, b_conv_w[j], b_conv_b[j], b_b_i[j],
                         b_b_f[j], b_norm_g[j], b, s)
        else:
            a = _gmlp(x2, norm1_g[i], c_in, j, c_ln_g[j], c_ln_b[j], c_w_s[j], _pad_lanes(c_b_s[j].T))
        x2 = _out_ffn(a, w_outs[kind], j, x2, norm2_g[i], w_gu, w_down, i, final_g,
                      final_norm=(i == depth - 1))
    return x2.reshape(b, s, d)
```

```python
import functools
import math

import jax
import jax.numpy as jnp
from jax import lax
from jax.experimental import pallas as pl
from jax.experimental.pallas import tpu as pltpu

F32 = jnp.float32
BF16 = jnp.bfloat16

EPS = 1e-6
LOG2E = math.log2(math.e)
LANES = 128
A_HEADS = 16
A_HEAD_DIM = 64
B_HEADS = 8
B_HEAD_DIM = 128
B_CONV = 4
C_GROUPS = 8
C_SPAN = 128
CHUNK = 64
MLSTM_CHUNK = 256
VMEM_LIMIT = 56 * 1024 * 1024


def _params(*sem, flags=None):
    return pltpu.CompilerParams(dimension_semantics=sem, vmem_limit_bytes=VMEM_LIMIT, flags=flags)


def _resident(shape):
    nd = len(shape)
    return pl.BlockSpec(shape, lambda *_: (0,) * nd, pipeline_mode=pl.Buffered(1))


def _layer_block(shape, layer, col_block=0):
    return pl.BlockSpec((None,) + tuple(shape), lambda *_: (layer, 0, col_block),
                        pipeline_mode=pl.Buffered(1))


def _rmsnorm(x, g):
    ms = jnp.mean(x * x, axis=-1, keepdims=True)
    return x * lax.rsqrt(ms + EPS) * g


def _sigmoid(x):
    return 0.5 + 0.5 * jnp.tanh(0.5 * x)


def _silu(x):
    h = 0.5 * x
    return h + h * jnp.tanh(h)


def _log_sigmoid(z):
    return jnp.minimum(z, 0.0) - jnp.log1p(jnp.exp(-jnp.abs(z)))


def _gelu_exact(x):
    return 0.5 * x * (1.0 + lax.erf(x * (1.0 / math.sqrt(2.0))))


def _norm_proj_kernel(x_ref, g_ref, w_ref, wg_ref, o_ref, og_ref, *, col_chunk):
    n = w_ref.shape[1]
    tm = x_ref.shape[0]
    half = tm // 2
    for r0 in range(0, tm, half):
        rs = slice(r0, r0 + half)
        hn = _rmsnorm(x_ref[rs, :], g_ref[...]).astype(BF16)
        for c0 in range(0, n, col_chunk):
            y = jnp.dot(hn, w_ref[:, c0:c0 + col_chunk].astype(BF16), preferred_element_type=F32)
            o_ref[rs, c0:c0 + col_chunk] = y.astype(o_ref.dtype)
        og_ref[rs, :] = jnp.dot(hn, wg_ref[...], preferred_element_type=F32)


def _norm_proj(x2, g, w_all, layer, n, wg, *, tm=512, col_chunk=512):
    t, d = x2.shape
    return pl.pallas_call(
        functools.partial(_norm_proj_kernel, col_chunk=col_chunk),
        grid=(t // tm,),
        in_specs=[pl.BlockSpec((tm, d), lambda i: (i, 0)),
                  _resident((1, d)), _layer_block((d, n), layer), _resident((d, LANES))],
        out_specs=[pl.BlockSpec((tm, n), lambda i: (i, 0)),
                   pl.BlockSpec((tm, LANES), lambda i: (i, 0))],
        out_shape=[jax.ShapeDtypeStruct((t, n), BF16),
                   jax.ShapeDtypeStruct((t, LANES), F32)],
        compiler_params=_params("parallel"),
        name="norm_proj",
    )(x2, g.reshape(1, d), w_all, wg)


def _norm_proj_a_kernel(x_ref, g_ref, w_ref, wvt_ref, wg_ref, o_ref, vt_ref, og_ref, *, col_chunk,
                        lead_cols, lead_scale):
    n = w_ref.shape[1]
    d_v = wvt_ref.shape[0]
    tm = x_ref.shape[0]
    half = tm // 2
    for r0 in range(0, tm, half):
        rs = slice(r0, r0 + half)
        hn = _rmsnorm(x_ref[rs, :], g_ref[...]).astype(BF16)
        for c0 in range(0, n, col_chunk):
            y = jnp.dot(hn, w_ref[:, c0:c0 + col_chunk].astype(BF16), preferred_element_type=F32)
            if c0 + col_chunk <= lead_cols:
                y = y * lead_scale
            o_ref[rs, c0:c0 + col_chunk] = y.astype(o_ref.dtype)
        for c0 in range(0, d_v, col_chunk):
            yt = lax.dot_general(wvt_ref[c0:c0 + col_chunk, :], hn, (((1,), (1,)), ((), ())),
                                 preferred_element_type=F32)
            vt_ref[c0:c0 + col_chunk, rs] = yt.astype(vt_ref.dtype)
        og_ref[rs, :] = jnp.dot(hn, wg_ref[...], preferred_element_type=F32)


def _norm_proj_a(x2, g, w_all, layer, n, wvt, wg, *, lead_cols, lead_scale, tm=512, col_chunk=512):
    t, d = x2.shape
    d_v = wvt.shape[0]
    return pl.pallas_call(
        functools.partial(_norm_proj_a_kernel, col_chunk=col_chunk, lead_cols=lead_cols,
                          lead_scale=lead_scale),
        grid=(t // tm,),
        in_specs=[pl.BlockSpec((tm, d), lambda i: (i, 0)),
                  _resident((1, d)), _layer_block((d, n), layer), _resident((d_v, d)),
                  _resident((d, LANES))],
        out_specs=[pl.BlockSpec((tm, n), lambda i: (i, 0)),
                   pl.BlockSpec((d_v, tm), lambda i: (0, i)),
                   pl.BlockSpec((tm, LANES), lambda i: (i, 0))],
        out_shape=[jax.ShapeDtypeStruct((t, n), BF16),
                   jax.ShapeDtypeStruct((d_v, t), BF16),
                   jax.ShapeDtypeStruct((t, LANES), F32)],
        compiler_params=_params("parallel"),
        name="norm_proj_a",
    )(x2, g.reshape(1, d), w_all, wvt, wg)


def _out_ffn_kernel(a_ref, wo_ref, x_ref, g_ref, wg_ref, wu_ref, wd_ref, fg_ref, o_ref, *,
                    ff_chunk, final_norm):
    x1 = x_ref[...] + jnp.dot(a_ref[...], wo_ref[...], preferred_element_type=F32)
    hn = _rmsnorm(x1, g_ref[...]).astype(BF16)
    d_ff = wg_ref.shape[1]
    acc = x1
    for c0 in range(0, d_ff, ff_chunk):
        gate = jnp.dot(hn, wg_ref[:, c0:c0 + ff_chunk].astype(BF16), preferred_element_type=F32)
        up = jnp.dot(hn, wu_ref[:, c0:c0 + ff_chunk].astype(BF16), preferred_element_type=F32)
        h = (_silu(gate) * up).astype(BF16)
        acc = acc + jnp.dot(h, wd_ref[c0:c0 + ff_chunk, :].astype(BF16), preferred_element_type=F32)
    if final_norm:
        acc = _rmsnorm(acc, fg_ref[...])
    o_ref[...] = acc


def _out_ffn(a, wo_all, wo_layer, x2, g, w_gu, w_down, layer, final_g, *, final_norm, tm=512,
             ff_chunk=256):
    t, d = x2.shape
    k = a.shape[1]
    d_ff = w_down.shape[1]
    return pl.pallas_call(
        functools.partial(_out_ffn_kernel, ff_chunk=ff_chunk, final_norm=final_norm),
        grid=(t // tm,),
        in_specs=[pl.BlockSpec((tm, k), lambda i: (i, 0)), _layer_block((k, d), wo_layer),
                  pl.BlockSpec((tm, d), lambda i: (i, 0)), _resident((1, d)),
                  _layer_block((d, d_ff), layer, 0), _layer_block((d, d_ff), layer, 1),
                  _layer_block((d_ff, d), layer), _resident((1, d))],
        out_specs=pl.BlockSpec((tm, d), lambda i: (i, 0)),
        out_shape=jax.ShapeDtypeStruct((t, d), F32),
        compiler_params=_params("parallel"),
        name="out_ffn",
    )(a, wo_all, x2, g.reshape(1, d), w_gu, w_gu, w_down, final_g.reshape(1, d))


def _gate_scan_kernel(z_ref, bias_ref, tri_ref, o_ref, *, n_forget, segment, split_scale):
    rows = tri_ref.shape[0]
    lane = lax.broadcasted_iota(jnp.int32, (rows, LANES), 1)
    forget = lane < n_forget
    tri = tri_ref[...]
    run = None
    for r0 in range(0, z_ref.shape[1], rows):
        z = z_ref[0, r0:r0 + rows] + bias_ref[...]
        lf = jnp.where(forget, _log_sigmoid(z), 0.0)
        cum = jnp.zeros_like(lf)
        rest = lf
        for _ in range(3):
            part = rest.astype(BF16)
            cum = cum + jnp.dot(tri, part, preferred_element_type=F32)
            rest = rest - part.astype(F32)
        if segment > rows:
            if r0 % segment != 0:
                cum = cum + run
            run = cum[rows - 1:rows, :]
        rs = slice(r0, r0 + rows)
        if split_scale is None:
            o_ref[0, rs] = jnp.where(forget, cum, z)
        else:
            group = n_forget // 3
            val = cum * split_scale
            hi = val.astype(BF16).astype(F32)
            mid = (val - hi).astype(BF16).astype(F32)
            lo = val - hi - mid
            o_ref[0, rs] = jnp.where(lane < group, hi,
                                     jnp.where(lane < 2 * group, mid, lo)).astype(o_ref.dtype)


def _gate_scan(z, bias, *, n_forget, segment, split_scale=None):
    b, s, _ = z.shape
    rows = C_SPAN
    r = jnp.arange(rows)
    seg = min(segment, rows)
    tri = ((r[:, None] >= r[None, :]) & (r[:, None] // seg == r[None, :] // seg)).astype(BF16)
    blk = pl.BlockSpec((1, s, LANES), lambda i: (i, 0, 0))
    return pl.pallas_call(
        functools.partial(_gate_scan_kernel, n_forget=n_forget, segment=segment,
                          split_scale=split_scale),
        grid=(b,),
        in_specs=[blk, _resident((1, LANES)), _resident((rows, rows))],
        out_specs=blk,
        out_shape=jax.ShapeDtypeStruct((b, s, LANES), F32 if split_scale is None else BF16),
        compiler_params=_params("parallel"),
        name="gate_scan",
    )(z, bias, tri)


def _fox_t_kernel(q_ref, k_ref, cp_ref, vt_ref, o_ref, acc_ref, s_ref, *, tq, tk):
    i = pl.program_id(2)
    hd = A_HEAD_DIM
    ones_row = (hd, 0)

    lane = lax.broadcasted_iota(jnp.int32, (tq, LANES), 1)
    row_v = lax.broadcasted_iota(jnp.int32, (LANES, tk), 0)
    q = q_ref[0].astype(F32)
    q_aug = []
    for h in range(2):
        in_head = (lane >= h * hd) & (lane < (h + 1) * hd)
        head = 2 * pl.program_id(1) + h
        bias_sel = (lane < 3 * A_HEADS) & ((lane & (A_HEADS - 1)) == head)
        q_aug.append(jnp.concatenate(
            [jnp.where(in_head, q, 0.0), jnp.where(bias_sel, -1.0, 0.0)], axis=1).astype(BF16))
    acc_ref[...] = jnp.zeros_like(acc_ref)

    def scores(k0, c0=0):
        kb = jnp.concatenate([k_ref[0, pl.ds(k0, tk), :], cp_ref[0, pl.ds(k0, tk), :]], axis=1)
        return [lax.dot_general(kb, q_aug[h][c0:], (((1,), (1,)), ((), ())),
                                preferred_element_type=F32) for h in range(2)]

    def update(ms, ss, k0, c0=0, masked=False):
        out = []
        for h in range(2):
            s = ss[h]
            if masked:
                row = lax.broadcasted_iota(jnp.int32, (tk, tq - c0), 0)
                col = lax.broadcasted_iota(jnp.int32, (tk, tq - c0), 1)
                s = jnp.where(row <= col, s, -jnp.inf)
            m_old = ms[h][:, c0:]
            m_new = jnp.maximum(m_old, jnp.max(s, axis=0, keepdims=True))
            alpha = jnp.exp2(m_old - m_new)
            p = jnp.exp2(s - m_new).astype(BF16)
            vt = vt_ref[:, pl.ds(k0, tk)]
            vt = jnp.where(row_v == ones_row[h], jnp.ones_like(vt), vt)
            acc_ref[h, :, c0:] = alpha * acc_ref[h, :, c0:] + jnp.dot(
                vt, p, preferred_element_type=F32)
            out.append(m_new if c0 == 0 else jnp.concatenate([ms[h][:, :c0], m_new], axis=1))
        return tuple(out)

    def start(blk):
        return pl.multiple_of(blk * tk, tk)

    def trip(t, ms):
        s_odd = scores(start(2 * t + 1))
        ms = update(ms, [s_ref[h] for h in range(2)], start(2 * t))
        s_next = scores(start(2 * t + 2))
        for h in range(2):
            s_ref[h] = s_next[h]
        return update(ms, s_odd, start(2 * t + 1))

    per_tile = tq // tk
    assert per_tile % 2 == 0
    s_first = scores(start(0))
    for h in range(2):
        s_ref[h] = s_first[h]
    ms = (jnp.full((1, tq), -jnp.inf, F32),) * 2
    ms = lax.fori_loop(0, i * (per_tile // 2), trip, ms)
    first = i * per_tile
    lows = [scores(start(first + jj), c0=jj * tk) for jj in range(1, per_tile)]
    ms = update(ms, [s_ref[h] for h in range(2)], start(first), masked=True)
    for jj in range(1, per_tile):
        ms = update(ms, lows[jj - 1], start(first + jj), c0=jj * tk, masked=True)
    outs = []
    for h in range(2):
        a = acc_ref[h]
        outs.append((a * (1.0 / a[ones_row[h]:ones_row[h] + 1, :])).T)
    o_ref[0] = jnp.where(lane < hd, outs[0], outs[1]).astype(o_ref.dtype)


def _fox_attention_t(qk, cparts, vt, *, tq=2048, tk=256):
    b, s, d2 = qk.shape
    tq = min(tq, s)
    d = d2 // 2
    nblk = d // LANES
    return pl.pallas_call(
        functools.partial(_fox_t_kernel, tq=tq, tk=tk),
        grid=(b, nblk, s // tq),
        in_specs=[pl.BlockSpec((1, tq, LANES), lambda bi, p, i: (bi, i, p)),
                  pl.BlockSpec((1, s, LANES), lambda bi, p, i: (bi, 0, nblk + p)),
                  pl.BlockSpec((1, s, LANES), lambda bi, p, i: (bi, 0, 0)),
                  pl.BlockSpec((LANES, s), lambda bi, p, i: (p, bi))],
        out_specs=pl.BlockSpec((1, tq, LANES), lambda bi, p, i: (bi, i, p)),
        out_shape=jax.ShapeDtypeStruct((b, s, d), BF16),
        scratch_shapes=[pltpu.VMEM((2, LANES, tq), F32), pltpu.VMEM((2, tk, tq), F32)],
        compiler_params=_params("parallel", "parallel", "arbitrary"),
        name="fox_attention",
    )(qk, qk, cparts, vt)


def _mlstm_kernel(q_ref, k_ref, v_ref, o_ref, gcol_ref, grow_ref, cw_ref, cb_ref, ng_ref, sh_ref,
                  out_ref, c_ref, m_ref, tailq_ref, tailk_ref, shq_ref, shk_ref, *, chunk):
    L = chunk
    d = B_HEAD_DIM
    width = B_HEADS * d
    taps = B_CONV - 1

    @pl.when(pl.program_id(1) == 0)
    def _():
        c_ref[...] = jnp.zeros_like(c_ref)
        m_ref[...] = jnp.zeros_like(m_ref)
        tailq_ref[...] = jnp.zeros_like(tailq_ref)
        tailk_ref[...] = jnp.zeros_like(tailk_ref)

    shq_ref[...] = jnp.dot(sh_ref[...], q_ref[0], preferred_element_type=F32)
    shk_ref[...] = jnp.dot(sh_ref[...], k_ref[0], preferred_element_type=F32)

    def conv_silu(x_ref, sh_x_ref, tail_ref, col0, sl):
        cs = slice(col0 + sl.start, col0 + sl.stop)
        x = x_ref[0, :, sl].astype(F32)
        y = cb_ref[:, cs] + cw_ref[taps:taps + 1, cs] * x
        for j in range(taps):
            y = y + cw_ref[taps - 1 - j:taps - j, cs] * sh_x_ref[j * L:(j + 1) * L, sl]
        slab = jnp.concatenate([tail_ref[:, sl], jnp.zeros((8, d), F32)], axis=0)
        head = y[0:8]
        for j in range(taps):
            head = head + cw_ref[taps - 1 - j:taps - j, cs] * slab[8 - (j + 1):16 - (j + 1)]
        return _silu(jnp.concatenate([head, y[8:]], axis=0))

    gcol = gcol_ref[0]
    grow = grow_ref[0, 0]
    row = lax.broadcasted_iota(jnp.int32, (L, L), 0)
    col = lax.broadcasted_iota(jnp.int32, (L, L), 1)
    tri = row >= col
    ones = jnp.ones((L, d), BF16)
    wide = lambda t: jnp.concatenate([t] * (L // d), axis=1)

    for h in range(B_HEADS):
        sl = slice(h * d, (h + 1) * d)
        qh = conv_silu(q_ref, shq_ref, tailq_ref, 0, sl).astype(BF16)
        kf = conv_silu(k_ref, shk_ref, tailk_ref, width, sl) * (d ** -0.5)
        kh = kf.astype(BF16)
        vaug = jnp.concatenate([v_ref[0, :, sl], ones], axis=1)
        bc = jnp.broadcast_to(gcol[:, h:h + 1], (L, d))
        ig = jnp.broadcast_to(gcol[:, B_HEADS + h:B_HEADS + h + 1], (L, d))
        br = grow[h:h + 1, :]
        igr = grow[B_HEADS + h:B_HEADS + h + 1, :]
        m_old = m_ref[h:h + 1, :]

        log_d = jnp.where(tri, wide(bc) - br + igr, -jnp.inf)
        m_inter = bc + m_old
        m_t = jnp.maximum(m_inter, jnp.max(log_d, axis=-1, keepdims=True))
        dm = jnp.exp(log_d - wide(m_t))
        a = jnp.exp(m_inter - m_t)
        sqk = lax.dot_general(qh, kh, (((1,), (1,)), ((), ())), preferred_element_type=F32) * dm
        c_aug = c_ref[h]
        qc = jnp.dot(qh, c_aug.astype(BF16), preferred_element_type=F32)
        sv = jnp.dot(sqk.astype(BF16), vaug, preferred_element_type=F32)
        num = a * qc[:, :d] + sv[:, :d]
        den = a * qc[:, d:] + sv[:, d:]
        hc = num * (1.0 / jnp.maximum(jnp.abs(den), jnp.exp(-m_t)))
        hc = hc * lax.rsqrt(jnp.mean(hc * hc, axis=-1, keepdims=True) + EPS)
        gate = _sigmoid(o_ref[0, :, sl].astype(F32))
        out_ref[0, :, sl] = (hc * ng_ref[:, sl] * gate).astype(out_ref.dtype)

        b_last = bc[L - 1:L, :]
        g = b_last - bc + ig
        m_new = jnp.maximum(b_last + m_old, jnp.max(g, axis=0, keepdims=True))
        w = jnp.exp(g - m_new)
        decay = jnp.exp(b_last + m_old - m_new)
        kw = (kf * w).astype(BF16)
        upd = lax.dot_general(kw, vaug, (((0,), (0,)), ((), ())), preferred_element_type=F32)
        c_ref[h] = jnp.concatenate([decay, decay], axis=1) * c_aug + upd
        m_ref[h:h + 1, :] = m_new

    tailq_ref[...] = q_ref[0, L - 8:L, :].astype(F32)
    tailk_ref[...] = k_ref[0, L - 8:L, :].astype(F32)


def _mlstm(proj, gcol, grow, conv_w, conv_b, norm_g, *, chunk):
    b, s, d4 = proj.shape
    d = d4 // 4
    nc = s // chunk
    blk = lambda c0: pl.BlockSpec((1, chunk, d), lambda bi, c: (bi, c, c0))
    taps = B_CONV - 1
    r = jnp.arange(taps * chunk)
    shift = (r[:, None] % chunk - r[:, None] // chunk - 1 == jnp.arange(chunk)[None, :]).astype(BF16)
    return pl.pallas_call(
        functools.partial(_mlstm_kernel, chunk=chunk),
        grid=(b, nc),
        in_specs=[blk(0), blk(1), blk(2), blk(3),
                  pl.BlockSpec((1, chunk, LANES), lambda bi, c: (bi, c, 0)),
                  pl.BlockSpec((1, 1, 2 * B_HEADS, chunk), lambda bi, c: (bi, c, 0, 0)),
                  _resident((B_CONV, 2 * d)), _resident((1, 2 * d)), _resident((1, d)),
                  _resident((taps * chunk, chunk))],
        out_specs=pl.BlockSpec((1, chunk, d), lambda bi, c: (bi, c, 0)),
        out_shape=jax.ShapeDtypeStruct((b, s, d), BF16),
        scratch_shapes=[pltpu.VMEM((B_HEADS, B_HEAD_DIM, 2 * B_HEAD_DIM), F32),
                        pltpu.VMEM((B_HEADS, LANES), F32),
                        pltpu.VMEM((8, d), F32), pltpu.VMEM((8, d), F32),
                        pltpu.VMEM((taps * chunk, d), F32), pltpu.VMEM((taps * chunk, d), F32)],
        compiler_params=_params("parallel", "arbitrary"),
        name="mlstm",
    )(proj, proj, proj, proj, gcol, grow, conv_w, conv_b.reshape(1, 2 * d), norm_g.reshape(1, d),
      shift)


def _gmlp_kernel(x_ref, g_ref, wu_ref, wv_ref, lng_ref, lnb_ref, ws_ref, bs_ref, o_ref, v_ref):
    tm = x_ref.shape[0]
    width = wv_ref.shape[1]
    gw = width // C_GROUPS
    hn = _rmsnorm(x_ref[...], g_ref[...]).astype(BF16)
    for c0 in range(0, width, gw):
        v_ref[:, c0:c0 + gw] = _gelu_exact(
            jnp.dot(hn, wv_ref[:, c0:c0 + gw].astype(BF16), preferred_element_type=F32))
    v = v_ref[...]
    mu = jnp.mean(v, axis=-1, keepdims=True)
    vc = v - mu
    var = jnp.mean(vc * vc, axis=-1, keepdims=True)
    vn = (vc * lax.rsqrt(var + EPS) * lng_ref[...] + lnb_ref[...]).astype(BF16)
    row = lax.broadcasted_iota(jnp.int32, (C_SPAN, C_SPAN), 0)
    col = lax.broadcasted_iota(jnp.int32, (C_SPAN, C_SPAN), 1)
    mask = (row // CHUNK) >= (col // CHUNK)
    for gi in range(C_GROUPS):
        cs = slice(gi * gw, (gi + 1) * gw)
        ws = jnp.where(mask, ws_ref[gi], 0.0).astype(BF16)
        bias = bs_ref[:, gi:gi + 1]
        u = _gelu_exact(jnp.dot(hn, wu_ref[:, cs].astype(BF16), preferred_element_type=F32))
        for r0 in range(0, tm, C_SPAN):
            s = jnp.dot(ws, vn[r0:r0 + C_SPAN, cs], preferred_element_type=F32) + bias
            o_ref[r0:r0 + C_SPAN, cs] = (u[r0:r0 + C_SPAN] * s).astype(o_ref.dtype)


def _gmlp(x2, g, w_in_all, layer, ln_g, ln_b, w_s, bs_t, *, tm=512):
    t, d = x2.shape
    width = w_in_all.shape[2] // 2
    wu = wv = w_in_all
    return pl.pallas_call(
        _gmlp_kernel,
        grid=(t // tm,),
        in_specs=[pl.BlockSpec((tm, d), lambda i: (i, 0)), _resident((1, d)),
                  _layer_block((d, width), layer, 0), _layer_block((d, width), layer, 1),
                  _resident((1, width)), _resident((1, width)),
                  _resident((C_GROUPS, C_SPAN, C_SPAN)), _resident((C_SPAN, LANES))],
        out_specs=pl.BlockSpec((tm, width), lambda i: (i, 0)),
        out_shape=jax.ShapeDtypeStruct((t, width), BF16),
        scratch_shapes=[pltpu.VMEM((tm, width), F32)],
        compiler_params=_params("parallel"),
        name="gmlp",
    )(x2, g.reshape(1, d), wu, wv, ln_g.reshape(1, width), ln_b.reshape(1, width), w_s, bs_t)


def _pad_lanes(a):
    return jnp.pad(a, [(0, 0)] * (a.ndim - 1) + [(0, LANES - a.shape[-1])])


def _mixer_a(x2, g, w_in, j, b_f, b, s):
    d = x2.shape[1]
    wvt = w_in[j, :, 2 * d:3 * d].T.astype(BF16)
    w_f = jnp.tile(w_in[j, :, 3 * d:], (1, 3))
    qk, vt, zf = _norm_proj_a(x2, g, w_in, j, 2 * d, wvt, _pad_lanes(w_f).astype(BF16),
                              lead_cols=d, lead_scale=A_HEAD_DIM ** -0.5 * LOG2E)
    cparts = _gate_scan(zf.reshape(b, s, LANES), _pad_lanes(jnp.tile(b_f, 3)[None, :]),
                        n_forget=3 * A_HEADS, segment=s, split_scale=LOG2E)
    return _fox_attention_t(qk.reshape(b, s, 2 * d), cparts, vt).reshape(b * s, d)


def _mixer_b(x2, g, w_in, j, conv_w, conv_b, b_i, b_f, norm_g, b, s):
    d = x2.shape[1]
    chunk = MLSTM_CHUNK
    w_gate = jnp.concatenate([w_in[j, :, 4 * d + B_HEADS:], w_in[j, :, 4 * d:4 * d + B_HEADS]], axis=1)
    proj, zg = _norm_proj(x2, g, w_in, j, 4 * d, _pad_lanes(w_gate).astype(BF16))
    bias = _pad_lanes(jnp.concatenate([b_f, b_i])[None, :])
    gcol = _gate_scan(zg.reshape(b, s, LANES), bias, n_forget=B_HEADS, segment=chunk)
    grow = gcol[..., :2 * B_HEADS].reshape(b, s // chunk, chunk, 2 * B_HEADS).transpose(0, 1, 3, 2)
    hs = _mlstm(proj.reshape(b, s, 4 * d), gcol, grow, conv_w, conv_b, norm_g, chunk=chunk)
    return hs.reshape(b * s, d)


def kernel(x, norm1_g, norm2_g, final_g, a_w_in, a_b_f, a_w_out, b_w_in, b_conv_w, b_conv_b,
           b_b_i, b_b_f, b_norm_g, b_w_out, c_w_in, c_ln_g, c_ln_b, c_w_s, c_b_s, c_w_out,
           ffn_w_gu, ffn_w_down):
    b, s, d = x.shape
    depth = norm1_g.shape[0]
    w_outs = (a_w_out.astype(BF16), b_w_out.astype(BF16), c_w_out.astype(BF16))
    x2 = x.reshape(b * s, d)
    for i in range(depth):
        kind, j = i % 3, i // 3
        if kind == 0:
            a = _mixer_a(x2, norm1_g[i], a_w_in, j, a_b_f[j], b, s)
        elif kind == 1:
            a = _mixer_b(x2, norm1_g[i], b_w_in, j, b_conv_w[j], b_conv_b[j], b_b_i[j],
                         b_b_f[j], b_norm_g[j], b, s)
        else:
            a = _gmlp(x2, norm1_g[i], c_w_in, j, c_ln_g[j], c_ln_b[j], c_w_s[j],
                      _pad_lanes(c_b_s[j].T))
        x2 = _out_ffn(a, w_outs[kind], j, x2, norm2_g[i], ffn_w_gu, ffn_w_down, i, final_g,
                      final_norm=(i == depth - 1))
    return x2.reshape(b, s, d)
```
